```python
import math
import jax, jax.numpy as jnp
from jax import lax
import numpy as np

D_MODEL = 1024
BATCH = 4
SEQ = 8192
DEPTH = 2

D_RNN = D_MODEL
RNN_HEADS = 8
RNN_BLOCK = D_RNN // RNN_HEADS
RNN_CONV = 4
RG_C = 8.0
ATT_HEADS = 8
HEAD_DIM = 128
ATT_WIDTH = ATT_HEADS * HEAD_DIM
Q_BLOCK = 128
D_FF = 3 * D_MODEL
FFN_CONV = 3
EPS = 1e-6
IN_SPLITS = (D_RNN, D_RNN, ATT_WIDTH, ATT_WIDTH, ATT_WIDTH, D_MODEL, D_MODEL)
D_IN = sum(IN_SPLITS)

kernel_name = "hybrid_rglru_stickbreak_adaln_convffn"


def rmsnorm(x, g):
    x32 = x.astype(jnp.float32)
    y = x32 * lax.rsqrt(jnp.mean(x32 * x32, axis=-1, keepdims=True) + EPS)
    return (y * g.astype(jnp.float32)).astype(x.dtype)


def causal_dwconv(x, w, b):
    k_w = w.shape[0]
    s = x.shape[1]
    xp = jnp.pad(x, ((0, 0), (k_w - 1, 0), (0, 0)))
    out = b + w[0] * xp[:, 0:s]
    for k in range(1, k_w):
        out = out + w[k] * xp[:, k:k + s]
    return out


def rg_lru(x, wa, ba, wx, bx, lam):
    b_, s_, c_ = x.shape
    xh = x.reshape(b_, s_, RNN_HEADS, RNN_BLOCK)
    r = jax.nn.sigmoid(jnp.einsum('bshi,hij->bshj', xh, wa).reshape(b_, s_, c_) + ba)
    i = jax.nn.sigmoid(jnp.einsum('bshi,hij->bshj', xh, wx).reshape(b_, s_, c_) + bx)
    log_a = -RG_C * r.astype(jnp.float32) * jax.nn.softplus(-lam.astype(jnp.float32))
    a = jnp.exp(log_a)
    mult = jnp.sqrt(-jnp.expm1(2.0 * log_a))
    u = mult * (i * x).astype(jnp.float32)

    def step(h, inp):
        a_t, u_t = inp
        h = a_t * h + u_t
        return h, h

    h0 = jnp.zeros((b_, c_), jnp.float32)
    _, hs = lax.scan(step, h0, (jnp.swapaxes(a, 0, 1), jnp.swapaxes(u, 0, 1)))
    return jnp.swapaxes(hs, 0, 1).astype(x.dtype)


def head_rmsnorm(x, g):
    return rmsnorm(x, g)


def stick_breaking_attention(q, k, v):
    b_, s_, h_, d_ = q.shape
    nb = s_ // Q_BLOCK
    scale = 1.0 / math.sqrt(d_)
    kh = jnp.transpose(k, (0, 2, 1, 3))
    vh = jnp.transpose(v, (0, 2, 1, 3))
    qb = jnp.transpose(q.reshape(b_, nb, Q_BLOCK, h_, d_), (1, 0, 3, 2, 4))
    key_pos = jnp.arange(s_)

    def one_block(args):
        q_blk, blk = args
        q_pos = blk * Q_BLOCK + jnp.arange(Q_BLOCK)
        mask = key_pos[None, :] < q_pos[:, None]
        z = jnp.einsum('bhqd,bhkd->bhqk', q_blk, kh).astype(jnp.float32) * scale
        log_beta = jax.nn.log_sigmoid(z)
        log_1mb = jnp.where(mask, log_beta - z, 0.0)
        suffix = lax.cumsum(log_1mb, axis=3, reverse=True) - log_1mb
        attn = jnp.where(mask, jnp.exp(log_beta + suffix), 0.0)
        return jnp.einsum('bhqk,bhkd->bhqd', attn.astype(v.dtype), vh)

    out = lax.map(one_block, (qb, jnp.arange(nb)))
    return jnp.transpose(out, (1, 0, 3, 2, 4)).reshape(b_, s_, h_ * d_)


def setup_inputs(seed: int = 0) -> dict:
    key = jax.random.key(seed)
    ks = jax.random.split(key, 24)
    n = jax.random.normal
    L, D = DEPTH, D_MODEL
    u = jax.random.uniform(ks[10], (L, D_RNN), minval=0.9, maxval=0.999)
    a0 = u ** (1.0 / RG_C)
    lam = jnp.log(a0) - jnp.log1p(-a0)
    return {
        "x": n(ks[0], (BATCH, SEQ, D), jnp.float32),
        "c": n(ks[1], (BATCH, D), jnp.float32),
        "ada_w": n(ks[2], (L, D, 6 * D), jnp.float32) * (0.5 * D ** -0.5),
        "ada_b": n(ks[3], (L, 6 * D), jnp.float32) * 0.01,
        "norm1_g": 1.0 + 0.05 * n(ks[4], (L, D), jnp.float32),
        "w_in": n(ks[5], (L, D, D_IN), jnp.float32) * D ** -0.5,
        "conv_w": n(ks[6], (L, RNN_CONV, D_RNN), jnp.float32) * RNN_CONV ** -0.5,
        "conv_b": n(ks[7], (L, D_RNN), jnp.float32) * 0.01,
        "rg_wa": n(ks[8], (L, RNN_HEADS, RNN_BLOCK, RNN_BLOCK), jnp.float32) * RNN_BLOCK ** -0.5,
        "rg_ba": n(ks[9], (L, D_RNN), jnp.float32) * 0.01,
        "rg_wx": n(ks[11], (L, RNN_HEADS, RNN_BLOCK, RNN_BLOCK), jnp.float32) * RNN_BLOCK ** -0.5,
        "rg_bx": n(ks[12], (L, D_RNN), jnp.float32) * 0.01,
        "rg_lambda": lam.astype(jnp.float32),
        "q_norm_g": 1.0 + 0.05 * n(ks[13], (L, HEAD_DIM), jnp.float32),
        "k_norm_g": 1.0 + 0.05 * n(ks[14], (L, HEAD_DIM), jnp.float32),
        "w_out": n(ks[15], (L, D, D), jnp.float32) * D ** -0.5,
        "norm2_g": 1.0 + 0.05 * n(ks[16], (L, D), jnp.float32),
        "ffn_up": n(ks[17], (L, D, 2 * D_FF), jnp.float32) * D ** -0.5,
        "ffn_conv_w": n(ks[18], (L, FFN_CONV, 2 * D_FF), jnp.float32) * FFN_CONV ** -0.5,
        "ffn_conv_b": n(ks[19], (L, 2 * D_FF), jnp.float32) * 0.01,
        "ffn_down": n(ks[20], (L, D_FF, D), jnp.float32) * D_FF ** -0.5,
    }


def reference(x, c, ada_w, ada_b, norm1_g, w_in, conv_w, conv_b, rg_wa, rg_ba, rg_wx, rg_bx,
              rg_lambda, q_norm_g, k_norm_g, w_out, norm2_g, ffn_up, ffn_conv_w, ffn_conv_b,
              ffn_down):
    b_, s_, d_ = x.shape
    cuts = np.cumsum(IN_SPLITS)[:-1].tolist()
    for l in range(DEPTH):
        mod = c @ ada_w[l] + ada_b[l]
        sh1, sc1, gt1, sh2, sc2, gt2 = [m[:, None, :] for m in jnp.split(mod, 6, axis=-1)]

        h = rmsnorm(x, norm1_g[l]) * (1.0 + sc1) + sh1
        p = h @ w_in[l]
        xr, yr, q, k, v, ga, gb = jnp.split(p, cuts, axis=-1)

        xr = causal_dwconv(xr, conv_w[l], conv_b[l])
        y_a = rg_lru(xr, rg_wa[l], rg_ba[l], rg_wx[l], rg_bx[l], rg_lambda[l]) * jax.nn.gelu(yr)

        q = head_rmsnorm(q.reshape(b_, s_, ATT_HEADS, HEAD_DIM), q_norm_g[l])
        k = head_rmsnorm(k.reshape(b_, s_, ATT_HEADS, HEAD_DIM), k_norm_g[l])
        v = v.reshape(b_, s_, ATT_HEADS, HEAD_DIM)
        y_b = stick_breaking_attention(q, k, v)

        mix = jax.nn.sigmoid(ga) * y_a + jax.nn.sigmoid(gb) * y_b
        x = x + gt1 * (mix @ w_out[l])

        h2 = rmsnorm(x, norm2_g[l]) * (1.0 + sc2) + sh2
        up = causal_dwconv(h2 @ ffn_up[l], ffn_conv_w[l], ffn_conv_b[l])
        g_ff, v_ff = jnp.split(up, 2, axis=-1)
        x = x + gt2 * ((jax.nn.gelu(g_ff) * v_ff) @ ffn_down[l])
    return x
```

```python
import functools
import math

import jax
import jax.numpy as jnp
from jax import lax
from jax.experimental import pallas as pl
from jax.experimental.pallas import tpu as pltpu

F32 = jnp.float32
BF16 = jnp.bfloat16

V7X_LANES = 128
V7X_SUBLANES = 8
V7X_VMEM_BYTES = 64 * 1024 * 1024

RNN_HEADS = 8
RNN_BLOCK = 128
RNN_CONV = 4
RG_C = 8.0
ATT_HEADS = 8
HEAD_DIM = 128
FFN_CONV = 3
EPS = 1e-6

F32_EXP_ZERO_BELOW = -104.0

TM_IN = 512
TM_FFN = 512
T_ATT = 256
FF_CHUNK = 512
MOD_TN = 1536


def _vmem_limit(nbytes):
    return int(min(nbytes, V7X_VMEM_BYTES - 4 * 1024 * 1024))


def _const_spec(shape):
    zeros = (0,) * len(shape)
    return pl.BlockSpec(shape, lambda *_: zeros, pipeline_mode=pl.Buffered(1))


def _gelu_tanh(x):
    c = math.sqrt(2.0 / math.pi)
    return 0.5 * x * (1.0 + jnp.tanh(c * (x + 0.044715 * (x * x * x))))


def _sigmoid(x):
    return 1.0 / (1.0 + jnp.exp(-x))


def _rms_scale(x):
    return lax.rsqrt(jnp.mean(x * x, axis=-1, keepdims=True) + EPS)


def _mod_kernel(c_ref, w_ref, b_ref, o_ref):
    acc = jnp.dot(c_ref[...].astype(BF16), w_ref[0].astype(BF16),
                  preferred_element_type=F32)
    o_ref[0] = acc + b_ref[0]


def _adaln_mod(c, ada_w, ada_b):
    depth, d, n = ada_w.shape
    b = c.shape[0]
    return pl.pallas_call(
        _mod_kernel,
        grid=(depth, n // MOD_TN),
        in_specs=[
            pl.BlockSpec((b, d), lambda l, j: (0, 0)),
            pl.BlockSpec((1, d, MOD_TN), lambda l, j: (l, 0, j)),
            pl.BlockSpec((1, 1, MOD_TN), lambda l, j: (l, 0, j)),
        ],
        out_specs=pl.BlockSpec((1, b, MOD_TN), lambda l, j: (l, 0, j)),
        out_shape=jax.ShapeDtypeStruct((depth, b, n), F32),
        compiler_params=pltpu.CompilerParams(
            dimension_semantics=("arbitrary", "arbitrary"),
            vmem_limit_bytes=_vmem_limit(32 * 1024 * 1024)),
        name="adaln_mod",
    )(c, ada_w, ada_b.reshape(depth, 1, n))


def _in_proj_kernel(x_ref, mod_ref, g1_ref, w_ref, cw_ref, cb_ref, wg_ref, ba_ref, bx_ref,
                    lam_ref, qg_ref, kg_ref,
                    q_ref, k_ref, v_ref, yag_ref, sgb_ref,
                    ext_ref, a_ref, u_ref, h_ref, *, tm, d):
    t_idx = pl.program_id(1)
    halo = V7X_SUBLANES

    @pl.when(t_idx == 0)
    def _():
        ext_ref[0:halo, :] = jnp.zeros((halo, d), F32)
        h_ref[...] = jnp.zeros((1, d), F32)

    x = x_ref[0]
    sh1 = mod_ref[0, 0:1, :]
    sc1 = mod_ref[0, 1:2, :]
    h = (x * _rms_scale(x) * g1_ref[...]) * (1.0 + sc1) + sh1
    hb = h.astype(BF16)

    def proj(col):
        return jnp.dot(hb, w_ref[:, col * d:(col + 1) * d], preferred_element_type=F32)

    ext_ref[halo:halo + tm, :] = proj(0)
    xc = cb_ref[...] + cw_ref[RNN_CONV - 1:RNN_CONV, :] * ext_ref[halo:halo + tm, :]
    for kk in range(RNN_CONV - 1):
        shift = RNN_CONV - 1 - kk
        xc = xc + cw_ref[kk:kk + 1, :] * ext_ref[halo - shift:halo - shift + tm, :]
    ext_ref[0:halo, :] = ext_ref[tm:tm + halo, :]

    decay = -RG_C * jnp.log(1.0 + jnp.exp(-lam_ref[...]))
    for hh in range(RNN_HEADS):
        sl = slice(hh * RNN_BLOCK, (hh + 1) * RNN_BLOCK)
        xh = xc[:, sl]
        ri = jnp.dot(xh.astype(BF16), wg_ref[hh], preferred_element_type=F32)
        r = _sigmoid(ri[:, :RNN_BLOCK] + ba_ref[:, sl])
        ig = _sigmoid(ri[:, RNN_BLOCK:] + bx_ref[:, sl])
        log_a = r * decay[:, sl]
        a = jnp.exp(log_a)
        a_ref[:, sl] = a
        u_ref[:, sl] = jnp.sqrt(1.0 - a * a) * (ig * xh)

    row = lax.broadcasted_iota(jnp.int32, (V7X_SUBLANES, d), 0)

    def scan_group(g, carry):
        base = pl.multiple_of(g * V7X_SUBLANES, V7X_SUBLANES)
        a = a_ref[pl.ds(base, V7X_SUBLANES), :]
        u = u_ref[pl.ds(base, V7X_SUBLANES), :]
        for dist in (1, 2, 4):
            keep = row >= dist
            a_prev = pltpu.roll(a, dist, axis=0)
            u_prev = pltpu.roll(u, dist, axis=0)
            u = jnp.where(keep, a * u_prev + u, u)
            a = jnp.where(keep, a * a_prev, a)
        hs = a * carry + u
        u_ref[pl.ds(base, V7X_SUBLANES), :] = hs
        return hs[V7X_SUBLANES - 1:V7X_SUBLANES, :]

    h_ref[...] = lax.fori_loop(0, tm // V7X_SUBLANES, scan_group, h_ref[...])

    y_a = u_ref[...] * _gelu_tanh(proj(1))
    yag_ref[0] = (_sigmoid(proj(5)) * y_a).astype(BF16)
    sgb_ref[0] = _sigmoid(proj(6)).astype(BF16)

    def head_norm(p, g_ref, o_ref):
        for hh in range(ATT_HEADS):
            sl = slice(hh * HEAD_DIM, (hh + 1) * HEAD_DIM)
            ph = p[:, sl]
            o_ref[0, :, sl] = (ph * _rms_scale(ph) * g_ref[...]).astype(BF16)

    head_norm(proj(2), qg_ref, q_ref)
    head_norm(proj(3), kg_ref, k_ref)
    v_ref[0] = proj(4).astype(BF16)


def _in_proj_rglru(x, mod, g1, w_in, conv_w, conv_b, w_gate, ba, bx, lam, qg, kg):
    b, s, d = x.shape
    tm = TM_IN
    tok_spec = pl.BlockSpec((1, tm, d), lambda bi, ti: (bi, ti, 0))
    act = jax.ShapeDtypeStruct((b, s, d), BF16)
    kern = functools.partial(_in_proj_kernel, tm=tm, d=d)
    return pl.pallas_call(
        kern,
        grid=(b, s // tm),
        in_specs=[
            tok_spec,
            pl.BlockSpec((1, 6, d), lambda bi, ti: (bi, 0, 0)),
            _const_spec((1, d)),
            _const_spec(w_in.shape),
            _const_spec(conv_w.shape),
            _const_spec((1, d)),
            _const_spec(w_gate.shape),
            _const_spec((1, d)),
            _const_spec((1, d)),
            _const_spec((1, d)),
            _const_spec((1, HEAD_DIM)),
            _const_spec((1, HEAD_DIM)),
        ],
        out_specs=[tok_spec] * 5,
        out_shape=[act] * 5,
        scratch_shapes=[
            pltpu.VMEM((tm + V7X_SUBLANES, d), F32),
            pltpu.VMEM((tm, d), F32),
            pltpu.VMEM((tm, d), F32),
            pltpu.VMEM((1, d), F32),
        ],
        compiler_params=pltpu.CompilerParams(
            dimension_semantics=("arbitrary", "arbitrary"),
            vmem_limit_bytes=_vmem_limit(60 * 1024 * 1024)),
        name="in_proj_rglru",
    )(x, mod, g1, w_in, conv_w, conv_b, w_gate, ba, bx, lam, qg, kg)


def _attn_kernel(q_ref, k_ref, v_ref, o_ref, acc_ref, r_ref, *, t, scale):
    i = pl.program_id(2)
    q = q_ref[0]
    row = lax.broadcasted_iota(jnp.int32, (t, t), 0)
    col = lax.broadcasted_iota(jnp.int32, (t, t), 1)
    below = row > col
    tri = below.astype(BF16)

    def block(j, on_diagonal):
        start = pl.multiple_of(j * t, t)
        kb = k_ref[0, pl.ds(start, t), :]
        vb = v_ref[0, pl.ds(start, t), :]
        z = lax.dot_general(q, kb, (((1,), (1,)), ((), ())),
                            preferred_element_type=F32) * scale
        sp = jnp.log(1.0 + jnp.exp(-jnp.abs(z)))
        log_beta = jnp.minimum(z, 0.0) - sp
        log_1mb = -jnp.maximum(z, 0.0) - sp
        if on_diagonal:
            log_1mb = jnp.where(below, log_1mb, 0.0)
        hi = log_1mb.astype(BF16)
        lo = (log_1mb - hi.astype(F32)).astype(BF16)
        suffix = (jnp.dot(hi, tri, preferred_element_type=F32)
                  + jnp.dot(lo, tri, preferred_element_type=F32))
        r_old = r_ref[...]
        attn = jnp.exp(log_beta + suffix + r_old)
        if on_diagonal:
            attn = jnp.where(below, attn, 0.0)
        acc_ref[...] += jnp.dot(attn.astype(BF16), vb, preferred_element_type=F32)
        r_new = r_old + suffix[:, 0:1] + log_1mb[:, 0:1]
        r_ref[...] = r_new
        return jnp.max((r_new >= F32_EXP_ZERO_BELOW).astype(jnp.int32))

    acc_ref[...] = jnp.zeros_like(acc_ref)
    r_ref[...] = jnp.zeros_like(r_ref)
    m0 = block(i, True)

    def cond(c):
        j, live = c
        return jnp.logical_and(j >= 0, live > 0)

    def body(c):
        j, _ = c
        return j - 1, block(j, False)

    lax.while_loop(cond, body, (i - 1, m0))
    o_ref[0] = acc_ref[...].astype(BF16)


def _stickbreak(q, k, v):
    b, s, d = q.shape
    t = T_ATT
    kern = functools.partial(_attn_kernel, t=t, scale=1.0 / math.sqrt(HEAD_DIM))
    q_spec = pl.BlockSpec((1, t, HEAD_DIM), lambda bi, hi, qi: (bi, qi, hi))
    kv_spec = pl.BlockSpec((1, s, HEAD_DIM), lambda bi, hi, qi: (bi, 0, hi))
    return pl.pallas_call(
        kern,
        grid=(b, ATT_HEADS, s // t),
        in_specs=[q_spec, kv_spec, kv_spec],
        out_specs=q_spec,
        out_shape=jax.ShapeDtypeStruct((b, s, d), BF16),
        scratch_shapes=[pltpu.VMEM((t, HEAD_DIM), F32), pltpu.VMEM((t, 1), F32)],
        compiler_params=pltpu.CompilerParams(
            dimension_semantics=("arbitrary", "arbitrary", "arbitrary"),
            vmem_limit_bytes=_vmem_limit(40 * 1024 * 1024)),
        name="stickbreak",
    )(q, k, v)


def _mix_ffn_kernel(x_ref, yag_ref, sgb_ref, yb_ref, mod_ref, wo_ref, g2_ref, wu_ref,
                    cw_ref, cb_ref, wd_ref, o_ref, ext_ref, hist_ref, *, tm, d, dff):
    t_idx = pl.program_id(1)
    halo = V7X_SUBLANES

    @pl.when(t_idx == 0)
    def _():
        hist_ref[...] = jnp.zeros_like(hist_ref)

    gt1 = mod_ref[0, 2:3, :]
    sh2 = mod_ref[0, 3:4, :]
    sc2 = mod_ref[0, 4:5, :]
    gt2 = mod_ref[0, 5:6, :]

    mix = yag_ref[0].astype(F32) + sgb_ref[0].astype(F32) * yb_ref[0].astype(F32)
    x1 = x_ref[0] + gt1 * jnp.dot(mix.astype(BF16), wo_ref[...], preferred_element_type=F32)
    h2 = ((x1 * _rms_scale(x1) * g2_ref[...]) * (1.0 + sc2) + sh2).astype(BF16)

    def conv_cols(c0):
        cs = slice(c0, c0 + FF_CHUNK)
        ext_ref[0:halo, :] = hist_ref[:, cs]
        ext_ref[halo:halo + tm, :] = jnp.dot(h2, wu_ref[:, cs], preferred_element_type=F32)
        out = cb_ref[:, cs] + cw_ref[FFN_CONV - 1:FFN_CONV, cs] * ext_ref[halo:halo + tm, :]
        for kk in range(FFN_CONV - 1):
            shift = FFN_CONV - 1 - kk
            out = out + cw_ref[kk:kk + 1, cs] * ext_ref[halo - shift:halo - shift + tm, :]
        hist_ref[:, cs] = ext_ref[tm:tm + halo, :]
        return out

    acc = jnp.zeros((tm, d), F32)
    for c in range(dff // FF_CHUNK):
        g_ff = conv_cols(c * FF_CHUNK)
        v_ff = conv_cols(dff + c * FF_CHUNK)
        act = (_gelu_tanh(g_ff) * v_ff).astype(BF16)
        acc = acc + jnp.dot(act, wd_ref[c * FF_CHUNK:(c + 1) * FF_CHUNK, :],
                            preferred_element_type=F32)
    o_ref[0] = x1 + gt2 * acc


def _mix_ffn(x, yag, sgb, yb, mod, w_out, g2, w_up, conv_w, conv_b, w_down):
    b, s, d = x.shape
    dff = w_down.shape[0]
    tm = TM_FFN
    tok_spec = pl.BlockSpec((1, tm, d), lambda bi, ti: (bi, ti, 0))
    kern = functools.partial(_mix_ffn_kernel, tm=tm, d=d, dff=dff)
    return pl.pallas_call(
        kern,
        grid=(b, s // tm),
        in_specs=[
            tok_spec, tok_spec, tok_spec, tok_spec,
            pl.BlockSpec((1, 6, d), lambda bi, ti: (bi, 0, 0)),
            _const_spec(w_out.shape),
            _const_spec((1, d)),
            _const_spec(w_up.shape),
            _const_spec(conv_w.shape),
            _const_spec((1, 2 * dff)),
            _const_spec(w_down.shape),
        ],
        out_specs=tok_spec,
        out_shape=jax.ShapeDtypeStruct((b, s, d), F32),
        scratch_shapes=[
            pltpu.VMEM((tm + V7X_SUBLANES, FF_CHUNK), F32),
            pltpu.VMEM((V7X_SUBLANES, 2 * dff), F32),
        ],
        compiler_params=pltpu.CompilerParams(
            dimension_semantics=("arbitrary", "arbitrary"),
            vmem_limit_bytes=_vmem_limit(60 * 1024 * 1024)),
        name="mix_ffn",
    )(x, yag, sgb, yb, mod, w_out, g2, w_up, conv_w, conv_b, w_down)


def kernel(x, c, ada_w, ada_b, norm1_g, w_in, conv_w, conv_b, rg_wa, rg_ba, rg_wx, rg_bx,
           rg_lambda, q_norm_g, k_norm_g, w_out, norm2_g, ffn_up, ffn_conv_w, ffn_conv_b,
           ffn_down):
    depth, d, _ = ada_w.shape
    b = x.shape[0]
    assert x.shape[1] % max(TM_IN, TM_FFN, T_ATT) == 0 and d == ATT_HEADS * HEAD_DIM

    mod_all = _adaln_mod(c, ada_w, ada_b).reshape(depth, b, 6, d)
    for l in range(depth):
        mod = mod_all[l]
        w_gate = jnp.concatenate([rg_wa[l], rg_wx[l]], axis=-1).astype(BF16)
        q, k, v, yag, sgb = _in_proj_rglru(
            x, mod, norm1_g[l][None], w_in[l].astype(BF16), conv_w[l], conv_b[l][None],
            w_gate, rg_ba[l][None], rg_bx[l][None], rg_lambda[l][None],
            q_norm_g[l][None], k_norm_g[l][None])
        yb = _stickbreak(q, k, v)
        x = _mix_ffn(x, yag, sgb, yb, mod, w_out[l].astype(BF16), norm2_g[l][None],
                     ffn_up[l].astype(BF16), ffn_conv_w[l], ffn_conv_b[l][None],
                     ffn_down[l].astype(BF16))
    return x
```

```python
import functools
import math

import jax
import jax.numpy as jnp
from jax import lax
from jax.experimental import pallas as pl
from jax.experimental.pallas import tpu as pltpu

F32 = jnp.float32
BF16 = jnp.bfloat16

V7X_LANES = 128
V7X_SUBLANES = 8
V7X_VMEM_BYTES = 64 * 1024 * 1024

RNN_HEADS = 8
RNN_BLOCK = 128
RNN_CONV = 4
RG_C = 8.0
ATT_HEADS = 8
HEAD_DIM = 128
FFN_CONV = 3
EPS = 1e-6

F32_EXP_ZERO_BELOW = -104.0

TM_IN = 512
TM_FFN = 512
T_ATT = 128
ATT_HEADS_PER_STEP = 8
FF_CHUNK = 512
MOD_TN = 1536


def _vmem_limit(nbytes):
    return int(min(nbytes, V7X_VMEM_BYTES - 4 * 1024 * 1024))


def _const_spec(shape):
    zeros = (0,) * len(shape)
    return pl.BlockSpec(shape, lambda *_: zeros, pipeline_mode=pl.Buffered(1))


def _gelu_tanh(x):
    c = math.sqrt(2.0 / math.pi)
    return 0.5 * x * (1.0 + jnp.tanh(c * (x + 0.044715 * (x * x * x))))


def _sigmoid(x):
    return 1.0 / (1.0 + jnp.exp(-x))


def _rms_scale(x):
    return lax.rsqrt(jnp.mean(x * x, axis=-1, keepdims=True) + EPS)


def _mod_kernel(c_ref, w_ref, b_ref, o_ref):
    acc = jnp.dot(c_ref[...].astype(BF16), w_ref[0].astype(BF16),
                  preferred_element_type=F32)
    o_ref[0] = acc + b_ref[0]


def _adaln_mod(c, ada_w, ada_b):
    depth, d, n = ada_w.shape
    b = c.shape[0]
    return pl.pallas_call(
        _mod_kernel,
        grid=(depth, n // MOD_TN),
        in_specs=[
            pl.BlockSpec((b, d), lambda l, j: (0, 0)),
            pl.BlockSpec((1, d, MOD_TN), lambda l, j: (l, 0, j)),
            pl.BlockSpec((1, 1, MOD_TN), lambda l, j: (l, 0, j)),
        ],
        out_specs=pl.BlockSpec((1, b, MOD_TN), lambda l, j: (l, 0, j)),
        out_shape=jax.ShapeDtypeStruct((depth, b, n), F32),
        compiler_params=pltpu.CompilerParams(
            dimension_semantics=("arbitrary", "arbitrary"),
            vmem_limit_bytes=_vmem_limit(32 * 1024 * 1024)),
        name="adaln_mod",
    )(c, ada_w, ada_b.reshape(depth, 1, n))


def _in_proj_kernel(x_ref, mod_ref, g1_ref, w_ref, cw_ref, cb_ref, wg_ref, ba_ref, bx_ref,
                    lam_ref, qg_ref, kg_ref,
                    q_ref, k_ref, v_ref, yag_ref, sgb_ref,
                    ext_ref, a_ref, u_ref, h_ref, *, tm, d):
    t_idx = pl.program_id(1)
    halo = V7X_SUBLANES

    @pl.when(t_idx == 0)
    def _():
        ext_ref[0:halo, :] = jnp.zeros((halo, d), F32)
        h_ref[...] = jnp.zeros((1, d), F32)

    x = x_ref[0]
    sh1 = mod_ref[0, 0:1, :]
    sc1 = mod_ref[0, 1:2, :]
    h = (x * _rms_scale(x) * g1_ref[...]) * (1.0 + sc1) + sh1
    hb = h.astype(BF16)

    half = d // 2

    def proj(col, part=None):
        c0 = col * d if part is None else col * d + part * half
        c1 = (col + 1) * d if part is None else c0 + half
        return jnp.dot(hb, w_ref[:, c0:c1], preferred_element_type=F32)

    def head_norm(p, g_ref, o_ref, first_head):
        for hh in range(p.shape[1] // HEAD_DIM):
            ph = p[:, hh * HEAD_DIM:(hh + 1) * HEAD_DIM]
            c0 = (first_head + hh) * HEAD_DIM
            o_ref[0, :, c0:c0 + HEAD_DIM] = (ph * _rms_scale(ph) * g_ref[...]).astype(BF16)

    ext_ref[halo:halo + tm, :] = proj(0)
    q_proj = proj(2)
    xc = cb_ref[...] + cw_ref[RNN_CONV - 1:RNN_CONV, :] * ext_ref[halo:halo + tm, :]
    for kk in range(RNN_CONV - 1):
        shift = RNN_CONV - 1 - kk
        xc = xc + cw_ref[kk:kk + 1, :] * ext_ref[halo - shift:halo - shift + tm, :]
    ext_ref[0:halo, :] = ext_ref[tm:tm + halo, :]
    head_norm(q_proj, qg_ref, q_ref, 0)

    pieces = [(3, 0), (3, 1), (4, 0), (4, 1), (6, 0), (6, 1), (1, 0), (5, 0), (1, 1), (5, 1)]
    y_half = {}

    def finish(piece, val):
        col, part = piece
        cs = slice(part * half, (part + 1) * half)
        if col == 3:
            head_norm(val, kg_ref, k_ref, part * (ATT_HEADS // 2))
        elif col == 4:
            v_ref[0, :, cs] = val.astype(BF16)
        elif col == 6:
            sgb_ref[0, :, cs] = _sigmoid(val).astype(BF16)
        elif col == 1:
            y_half[part] = _gelu_tanh(val)
        else:
            ext_ref[halo:halo + tm, cs] = y_half[part] * _sigmoid(val)

    decay = -RG_C * jnp.log(1.0 + jnp.exp(-lam_ref[...]))
    pending = None
    for hh in range(RNN_HEADS):
        sl = slice(hh * RNN_BLOCK, (hh + 1) * RNN_BLOCK)
        xh = xc[:, sl]
        ri = jnp.dot(xh.astype(BF16), wg_ref[hh], preferred_element_type=F32)
        current = (pieces[hh], proj(*pieces[hh]))
        r = _sigmoid(ri[:, :RNN_BLOCK] + ba_ref[:, sl])
        ig = _sigmoid(ri[:, RNN_BLOCK:] + bx_ref[:, sl])
        log_a = r * decay[:, sl]
        a = jnp.exp(log_a)
        a_ref[:, sl] = a
        u_ref[:, sl] = jnp.sqrt(1.0 - a * a) * (ig * xh)
        if pending is not None:
            finish(*pending)
        pending = current
    for piece in pieces[RNN_HEADS:]:
        current = (piece, proj(*piece))
        finish(*pending)
        pending = current
    finish(*pending)

    row = lax.broadcasted_iota(jnp.int32, (V7X_SUBLANES, d), 0)

    def scan_group(g, carry):
        base = pl.multiple_of(g * V7X_SUBLANES, V7X_SUBLANES)
        a = a_ref[pl.ds(base, V7X_SUBLANES), :]
        u = u_ref[pl.ds(base, V7X_SUBLANES), :]
        for dist in (1, 2, 4):
            keep = row >= dist
            a_prev = pltpu.roll(a, dist, axis=0)
            u_prev = pltpu.roll(u, dist, axis=0)
            u = jnp.where(keep, a * u_prev + u, u)
            a = jnp.where(keep, a * a_prev, a)
        hs = a * carry + u
        u_ref[pl.ds(base, V7X_SUBLANES), :] = hs
        return hs[V7X_SUBLANES - 1:V7X_SUBLANES, :]

    h_ref[...] = lax.fori_loop(0, tm // V7X_SUBLANES, scan_group, h_ref[...])

    yag_ref[0] = (u_ref[...] * ext_ref[halo:halo + tm, :]).astype(BF16)


def _in_proj_rglru(x, mod, g1, w_in, conv_w, conv_b, w_gate, ba, bx, lam, qg, kg):
    b, s, d = x.shape
    tm = TM_IN
    tok_spec = pl.BlockSpec((1, tm, d), lambda bi, ti: (bi, ti, 0))
    act = jax.ShapeDtypeStruct((b, s, d), BF16)
    kern = functools.partial(_in_proj_kernel, tm=tm, d=d)
    return pl.pallas_call(
        kern,
        grid=(b, s // tm),
        in_specs=[
            tok_spec,
            pl.BlockSpec((1, 6, d), lambda bi, ti: (bi, 0, 0)),
            _const_spec((1, d)),
            _const_spec(w_in.shape),
            _const_spec(conv_w.shape),
            _const_spec((1, d)),
            _const_spec(w_gate.shape),
            _const_spec((1, d)),
            _const_spec((1, d)),
            _const_spec((1, d)),
            _const_spec((1, HEAD_DIM)),
            _const_spec((1, HEAD_DIM)),
        ],
        out_specs=[tok_spec] * 5,
        out_shape=[act] * 5,
        scratch_shapes=[
            pltpu.VMEM((tm + V7X_SUBLANES, d), F32),
            pltpu.VMEM((tm, d), F32),
            pltpu.VMEM((tm, d), F32),
            pltpu.VMEM((1, d), F32),
        ],
        compiler_params=pltpu.CompilerParams(
            dimension_semantics=("arbitrary", "arbitrary"),
            vmem_limit_bytes=_vmem_limit(60 * 1024 * 1024)),
        name="in_proj_rglru",
    )(x, mod, g1, w_in, conv_w, conv_b, w_gate, ba, bx, lam, qg, kg)


def _attn_kernel(q_ref, k_ref, v_ref, o_ref, acc_ref, r_ref, *, t, heads, scale):
    i = pl.program_id(2)
    row = lax.broadcasted_iota(jnp.int32, (t, t), 0)
    col = lax.broadcasted_iota(jnp.int32, (t, t), 1)
    below = row > col
    tri = below.astype(BF16)

    def block(j, on_diagonal):
        start = pl.multiple_of(j * t, t)
        hs = range(heads)
        sls = [slice(g * HEAD_DIM, (g + 1) * HEAD_DIM) for g in hs]
        zs = [lax.dot_general(q_ref[0, :, sl], k_ref[0, pl.ds(start, t), sl],
                              (((1,), (1,)), ((), ())), preferred_element_type=F32)
              for sl in sls]
        log_betas, log_1mbs, parts = [], [], []
        for z_raw in zs:
            z = z_raw * scale
            log_beta = jnp.minimum(z, 0.0) - jnp.log(1.0 + jnp.exp(-jnp.abs(z)))
            log_1mb = log_beta - z
            if on_diagonal:
                log_1mb = jnp.where(below, log_1mb, 0.0)
            hi = log_1mb.astype(BF16)
            lo = (log_1mb - hi.astype(F32)).astype(BF16)
            log_betas.append(log_beta)
            log_1mbs.append(log_1mb)
            parts += [hi, lo]
        sums = jnp.dot(jnp.concatenate(parts, axis=0), tri, preferred_element_type=F32)
        attns, r_max = [], None
        for g in hs:
            suffix = sums[2 * g * t:(2 * g + 1) * t] + sums[(2 * g + 1) * t:(2 * g + 2) * t]
            r_old = r_ref[g]
            attn = jnp.exp(log_betas[g] + suffix + r_old)
            if on_diagonal:
                attn = jnp.where(below, attn, 0.0)
            attns.append(attn.astype(BF16))
            r_new = r_old + suffix[:, 0:1] + log_1mbs[g][:, 0:1]
            r_ref[g] = r_new
            r_max = r_new if r_max is None else jnp.maximum(r_max, r_new)
        for g in hs:
            acc_ref[:, sls[g]] += jnp.dot(attns[g], v_ref[0, pl.ds(start, t), sls[g]],
                                          preferred_element_type=F32)
        return jnp.max((r_max >= F32_EXP_ZERO_BELOW).astype(jnp.int32))

    acc_ref[...] = jnp.zeros_like(acc_ref)
    r_ref[...] = jnp.zeros_like(r_ref)
    m0 = block(i, True)

    def cond(c):
        j, live = c
        return jnp.logical_and(j >= 0, live > 0)

    def body(c):
        j, _ = c
        return j - 1, block(j, False)

    lax.while_loop(cond, body, (i - 1, m0))
    o_ref[0] = acc_ref[...].astype(BF16)


def _stickbreak(q, k, v):
    b, s, d = q.shape
    t = T_ATT
    hg = ATT_HEADS_PER_STEP
    kern = functools.partial(_attn_kernel, t=t, heads=hg, scale=1.0 / math.sqrt(HEAD_DIM))
    q_spec = pl.BlockSpec((1, t, hg * HEAD_DIM), lambda bi, hi, qi: (bi, qi, hi))
    kv_spec = pl.BlockSpec((1, s, hg * HEAD_DIM), lambda bi, hi, qi: (bi, 0, hi),
                           pipeline_mode=pl.Buffered(1))
    return pl.pallas_call(
        kern,
        grid=(b, ATT_HEADS // hg, s // t),
        in_specs=[q_spec, kv_spec, kv_spec],
        out_specs=q_spec,
        out_shape=jax.ShapeDtypeStruct((b, s, d), BF16),
        scratch_shapes=[pltpu.VMEM((t, hg * HEAD_DIM), F32), pltpu.VMEM((hg, t, 1), F32)],
        compiler_params=pltpu.CompilerParams(
            dimension_semantics=("arbitrary", "arbitrary", "arbitrary"),
            vmem_limit_bytes=_vmem_limit(48 * 1024 * 1024)),
        name="stickbreak",
    )(q, k, v)


def _mix_ffn_kernel(x_ref, yag_ref, sgb_ref, yb_ref, mod_ref, wo_ref, g2_ref, wu_ref,
                    cw_ref, cb_ref, wd_ref, o_ref, ext_ref, hist_ref, act_ref, *, tm, d, dff):
    t_idx = pl.program_id(1)
    halo = V7X_SUBLANES

    @pl.when(t_idx == 0)
    def _():
        hist_ref[...] = jnp.zeros_like(hist_ref)

    gt1 = mod_ref[0, 2:3, :]
    sh2 = mod_ref[0, 3:4, :]
    sc2 = mod_ref[0, 4:5, :]
    gt2 = mod_ref[0, 5:6, :]

    mix = yag_ref[0].astype(F32) + sgb_ref[0].astype(F32) * yb_ref[0].astype(F32)
    x1 = x_ref[0] + gt1 * jnp.dot(mix.astype(BF16), wo_ref[...], preferred_element_type=F32)
    h2 = ((x1 * _rms_scale(x1) * g2_ref[...]) * (1.0 + sc2) + sh2).astype(BF16)

    def col_slices(c):
        return (slice(c * FF_CHUNK, (c + 1) * FF_CHUNK),
                slice(dff + c * FF_CHUNK, dff + (c + 1) * FF_CHUNK))

    def up_proj(c):
        for part, cs in enumerate(col_slices(c)):
            buf = 2 * (c % 2) + part
            ext_ref[buf, 0:halo, :] = hist_ref[:, cs]
            ext_ref[buf, halo:halo + tm, :] = jnp.dot(h2, wu_ref[:, cs],
                                                      preferred_element_type=F32)
            hist_ref[:, cs] = ext_ref[buf, tm:tm + halo, :]

    def conv(c, part):
        cs = col_slices(c)[part]
        buf = 2 * (c % 2) + part
        out = cb_ref[:, cs] + cw_ref[FFN_CONV - 1:FFN_CONV, cs] * ext_ref[buf, halo:halo + tm, :]
        for kk in range(FFN_CONV - 1):
            shift = FFN_CONV - 1 - kk
            out = out + cw_ref[kk:kk + 1, cs] * ext_ref[buf, halo - shift:halo - shift + tm, :]
        return out

    def down_proj(c):
        return jnp.dot(act_ref[c % 2], wd_ref[c * FF_CHUNK:(c + 1) * FF_CHUNK, :],
                       preferred_element_type=F32)

    n_chunks = dff // FF_CHUNK
    acc = jnp.zeros((tm, d), F32)
    up_proj(0)
    for c in range(n_chunks):
        if c + 1 < n_chunks:
            up_proj(c + 1)
        if c > 0:
            acc = acc + down_proj(c - 1)
        act_ref[c % 2] = (_gelu_tanh(conv(c, 0)) * conv(c, 1)).astype(BF16)
    acc = acc + down_proj(n_chunks - 1)
    o_ref[0] = x1 + gt2 * acc


def _mix_ffn(x, yag, sgb, yb, mod, w_out, g2, w_up, conv_w, conv_b, w_down):
    b, s, d = x.shape
    dff = w_down.shape[0]
    tm = TM_FFN
    tok_spec = pl.BlockSpec((1, tm, d), lambda bi, ti: (bi, ti, 0))
    kern = functools.partial(_mix_ffn_kernel, tm=tm, d=d, dff=dff)
    return pl.pallas_call(
        kern,
        grid=(b, s // tm),
        in_specs=[
            tok_spec, tok_spec, tok_spec, tok_spec,
            pl.BlockSpec((1, 6, d), lambda bi, ti: (bi, 0, 0)),
            _const_spec(w_out.shape),
            _const_spec((1, d)),
            _const_spec(w_up.shape),
            _const_spec(conv_w.shape),
            _const_spec((1, 2 * dff)),
            _const_spec(w_down.shape),
        ],
        out_specs=tok_spec,
        out_shape=jax.ShapeDtypeStruct((b, s, d), F32),
        scratch_shapes=[
            pltpu.VMEM((4, tm + V7X_SUBLANES, FF_CHUNK), F32),
            pltpu.VMEM((V7X_SUBLANES, 2 * dff), F32),
            pltpu.VMEM((2, tm, FF_CHUNK), BF16),
        ],
        compiler_params=pltpu.CompilerParams(
            dimension_semantics=("arbitrary", "arbitrary"),
            vmem_limit_bytes=_vmem_limit(60 * 1024 * 1024)),
        name="mix_ffn",
    )(x, yag, sgb, yb, mod, w_out, g2, w_up, conv_w, conv_b, w_down)


def kernel(x, c, ada_w, ada_b, norm1_g, w_in, conv_w, conv_b, rg_wa, rg_ba, rg_wx, rg_bx,
           rg_lambda, q_norm_g, k_norm_g, w_out, norm2_g, ffn_up, ffn_conv_w, ffn_conv_b,
           ffn_down):
    depth, d, _ = ada_w.shape
    b = x.shape[0]
    assert x.shape[1] % max(TM_IN, TM_FFN, T_ATT) == 0 and d == ATT_HEADS * HEAD_DIM

    mod_all = _adaln_mod(c, ada_w, ada_b).reshape(depth, b, 6, d)
    for l in range(depth):
        mod = mod_all[l]
        w_gate = jnp.concatenate([rg_wa[l], rg_wx[l]], axis=-1).astype(BF16)
        q, k, v, yag, sgb = _in_proj_rglru(
            x, mod, norm1_g[l][None], w_in[l].astype(BF16), conv_w[l], conv_b[l][None],
            w_gate, rg_ba[l][None], rg_bx[l][None], rg_lambda[l][None],
            q_norm_g[l][None], k_norm_g[l][None])
        yb = _stickbreak(q, k, v)
        x = _mix_ffn(x, yag, sgb, yb, mod, w_out[l].astype(BF16), norm2_g[l][None],
                     ffn_up[l].astype(BF16), ffn_conv_w[l], ffn_conv_b[l][None],
                     ffn_down[l].astype(BF16))
    return x
```

```python
import functools
import math

import jax
import jax.numpy as jnp
from jax import lax
from jax.experimental import pallas as pl
from jax.experimental.pallas import tpu as pltpu

F32 = jnp.float32
BF16 = jnp.bfloat16

V7X_LANES = 128
V7X_SUBLANES = 8
V7X_VMEM_BYTES = 64 * 1024 * 1024

RNN_HEADS = 8
RNN_BLOCK = 128
RNN_CONV = 4
RG_C = 8.0
ATT_HEADS = 8
HEAD_DIM = 128
FFN_CONV = 3
EPS = 1e-6

F32_EXP_ZERO_BELOW = -104.0
OUT_OF_KEYS = -1e30

TM_IN = 512
TM_FFN = 512
T_ATT = 128
ATT_Q_BLOCKS_PER_STEP = 4
ATT_HEADS_PER_STEP = 8
FF_CHUNK = 512
MOD_TN = 1536


def _vmem_limit(nbytes):
    return int(min(nbytes, V7X_VMEM_BYTES - 4 * 1024 * 1024))


def _const_spec(shape):
    zeros = (0,) * len(shape)
    return pl.BlockSpec(shape, lambda *_: zeros, pipeline_mode=pl.Buffered(1))


def _gelu_tanh(x):
    c = math.sqrt(2.0 / math.pi)
    return 0.5 * x * (1.0 + jnp.tanh(c * (x + 0.044715 * (x * x * x))))


def _sigmoid(x):
    return 1.0 / (1.0 + jnp.exp(-x))


def _rms_scale(x):
    return lax.rsqrt(jnp.mean(x * x, axis=-1, keepdims=True) + EPS)


def _mod_kernel(c_ref, w_ref, b_ref, o_ref):
    acc = jnp.dot(c_ref[...].astype(BF16), w_ref[0].astype(BF16),
                  preferred_element_type=F32)
    o_ref[0] = acc + b_ref[0]


def _adaln_mod(c, ada_w, ada_b):
    depth, d, n = ada_w.shape
    b = c.shape[0]
    return pl.pallas_call(
        _mod_kernel,
        grid=(depth, n // MOD_TN),
        in_specs=[
            pl.BlockSpec((b, d), lambda l, j: (0, 0)),
            pl.BlockSpec((1, d, MOD_TN), lambda l, j: (l, 0, j)),
            pl.BlockSpec((1, 1, MOD_TN), lambda l, j: (l, 0, j)),
        ],
        out_specs=pl.BlockSpec((1, b, MOD_TN), lambda l, j: (l, 0, j)),
        out_shape=jax.ShapeDtypeStruct((depth, b, n), F32),
        compiler_params=pltpu.CompilerParams(
            dimension_semantics=("arbitrary", "arbitrary"),
            vmem_limit_bytes=_vmem_limit(32 * 1024 * 1024)),
        name="adaln_mod",
    )(c, ada_w, ada_b.reshape(depth, 1, n))


def _in_proj_kernel(x0_ref, xn_ref, mod_ref, g1_ref, w_ref, cw_ref, cb_ref, wg_ref, ba_ref,
                    bx_ref, lam_ref, qg_ref, kg_ref,
                    q_ref, k_ref, v_ref, yag_ref, sgb_ref,
                    ext_ref, hs_ref, h_ref, hb_ref, xc_ref, gy_ref, ri_ref, stage_ref,
                    *, tm, d):
    t_idx = pl.program_id(1)
    halo = V7X_SUBLANES
    cur = t_idx % 2
    nxt = 1 - cur
    sh1 = mod_ref[0, 0:1, :]
    sc1 = mod_ref[0, 1:2, :]

    def normed(x):
        return ((x * _rms_scale(x) * g1_ref[...]) * (1.0 + sc1) + sh1).astype(BF16)

    def conv_input(slot):
        ext_ref[halo:halo + tm, :] = jnp.dot(hb_ref[slot], w_ref[:, 0:d],
                                             preferred_element_type=F32)

    def conv(slot):
        xc = cb_ref[...] + cw_ref[RNN_CONV - 1:RNN_CONV, :] * ext_ref[halo:halo + tm, :]
        for kk in range(RNN_CONV - 1):
            shift = RNN_CONV - 1 - kk
            xc = xc + cw_ref[kk:kk + 1, :] * ext_ref[halo - shift:halo - shift + tm, :]
        ext_ref[0:halo, :] = ext_ref[tm:tm + halo, :]
        xc_ref[slot] = xc

    @pl.when(t_idx == 0)
    def _():
        ext_ref[0:halo, :] = jnp.zeros((halo, d), F32)
        h_ref[...] = jnp.zeros((1, d), F32)
        hb_ref[0] = normed(x0_ref[0])
        conv_input(0)
        conv(0)

    hb_ref[nxt] = normed(xn_ref[0])
    n_parts = d // stage_ref.shape[2]
    half = d // n_parts

    def head_norm(p, g_ref, o_ref, first_head):
        for hh in range(p.shape[1] // HEAD_DIM):
            ph = p[:, hh * HEAD_DIM:(hh + 1) * HEAD_DIM]
            c0 = (first_head + hh) * HEAD_DIM
            o_ref[0, :, c0:c0 + HEAD_DIM] = (ph * _rms_scale(ph) * g_ref[...]).astype(BF16)

    pieces = ([(col, part) for col in (2, 3, 4, 6) for part in range(n_parts)]
              + [(col, part) for part in range(n_parts) for col in (1, 5)])
    n_stage = stage_ref.shape[0]
    heads_per_piece = ATT_HEADS // n_parts

    def start(k):
        col, part = pieces[k]
        c0 = col * d + part * half
        stage_ref[k % n_stage] = jnp.dot(hb_ref[cur], w_ref[:, c0:c0 + half],
                                         preferred_element_type=F32)

    def finish(k):
        col, part = pieces[k]
        cs = slice(part * half, (part + 1) * half)
        val = stage_ref[k % n_stage]
        if col == 2:
            head_norm(val, qg_ref, q_ref, part * heads_per_piece)
        elif col == 3:
            head_norm(val, kg_ref, k_ref, part * heads_per_piece)
        elif col == 4:
            v_ref[0, :, cs] = val.astype(BF16)
        elif col == 6:
            sgb_ref[0, :, cs] = _sigmoid(val).astype(BF16)
        elif col == 1:
            gy_ref[:, cs] = _gelu_tanh(val)
        else:
            gy_ref[:, cs] = gy_ref[:, cs] * _sigmoid(val)

    per_head = len(pieces) // RNN_HEADS
    for k in range(per_head):
        start(k)
    conv_input(nxt)

    row = lax.broadcasted_iota(jnp.int32, (V7X_SUBLANES, RNN_BLOCK), 0)

    def scan_head(a_all, u_all, sl):
        carry = h_ref[:, sl]
        for g in range(tm // V7X_SUBLANES):
            rows = slice(g * V7X_SUBLANES, (g + 1) * V7X_SUBLANES)
            a, u = a_all[rows], u_all[rows]
            for dist in (1, 2, 4):
                keep = row >= dist
                a_prev = pltpu.roll(a, dist, axis=0)
                u_prev = pltpu.roll(u, dist, axis=0)
                u = jnp.where(keep, a * u_prev + u, u)
                a = jnp.where(keep, a * a_prev, a)
            hs = a * carry + u
            hs_ref[rows, sl] = hs
            carry = hs[V7X_SUBLANES - 1:V7X_SUBLANES, :]
        h_ref[:, sl] = carry

    decay = -RG_C * jnp.log(1.0 + jnp.exp(-lam_ref[...]))
    for hh in range(RNN_HEADS):
        sl = slice(hh * RNN_BLOCK, (hh + 1) * RNN_BLOCK)
        xh = xc_ref[cur, :, sl]
        ri_ref[hh % 2] = jnp.dot(xh.astype(BF16), wg_ref[hh], preferred_element_type=F32)
        for k in range(hh * per_head, (hh + 1) * per_head):
            if k + per_head < len(pieces):
                start(k + per_head)
            finish(k)
        if hh == 1:
            conv(nxt)
        r = _sigmoid(ri_ref[hh % 2, :, :RNN_BLOCK] + ba_ref[:, sl])
        ig = _sigmoid(ri_ref[hh % 2, :, RNN_BLOCK:] + bx_ref[:, sl])
        a = jnp.exp(r * decay[:, sl])
        scan_head(a, jnp.sqrt(1.0 - a * a) * (ig * xh), sl)

    yag_ref[0] = (hs_ref[...] * gy_ref[...]).astype(BF16)


def _in_proj_rglru(x, mod, g1, w_in, conv_w, conv_b, w_gate, ba, bx, lam, qg, kg):
    b, s, d = x.shape
    tm = TM_IN
    n_tiles = s // tm
    tok_spec = pl.BlockSpec((1, tm, d), lambda bi, ti: (bi, ti, 0))
    act = jax.ShapeDtypeStruct((b, s, d), BF16)
    kern = functools.partial(_in_proj_kernel, tm=tm, d=d)
    return pl.pallas_call(
        kern,
        grid=(b, n_tiles),
        in_specs=[
            pl.BlockSpec((1, tm, d), lambda bi, ti: (bi, 0, 0), pipeline_mode=pl.Buffered(1)),
            pl.BlockSpec((1, tm, d), lambda bi, ti: (bi, jnp.minimum(ti + 1, n_tiles - 1), 0)),
            pl.BlockSpec((1, 6, d), lambda bi, ti: (bi, 0, 0)),
            _const_spec((1, d)),
            _const_spec(w_in.shape),
            _const_spec(conv_w.shape),
            _const_spec((1, d)),
            _const_spec(w_gate.shape),
            _const_spec((1, d)),
            _const_spec((1, d)),
            _const_spec((1, d)),
            _const_spec((1, HEAD_DIM)),
            _const_spec((1, HEAD_DIM)),
        ],
        out_specs=[tok_spec] * 5,
        out_shape=[act] * 5,
        scratch_shapes=[
            pltpu.VMEM((tm + V7X_SUBLANES, d), F32),
            pltpu.VMEM((tm, d), F32),
            pltpu.VMEM((1, d), F32),
            pltpu.VMEM((2, tm, d), BF16),
            pltpu.VMEM((2, tm, d), F32),
            pltpu.VMEM((tm, d), F32),
            pltpu.VMEM((2, tm, 2 * RNN_BLOCK), F32),
            pltpu.VMEM((8, tm, d // 4), F32),
        ],
        compiler_params=pltpu.CompilerParams(
            dimension_semantics=("arbitrary", "arbitrary"),
            vmem_limit_bytes=_vmem_limit(60 * 1024 * 1024)),
        name="in_proj_rglru",
    )(x, x, mod, g1, w_in, conv_w, conv_b, w_gate, ba, bx, lam, qg, kg)


def _attn_kernel(q_ref, k_ref, v_ref, yag_ref, sgb_ref, o_ref, acc_ref, r_ref, z_ref, p_ref,
                 *, t, nq, heads, scale):
    i = pl.program_id(2)
    row = lax.broadcasted_iota(jnp.int32, (t, t), 0)
    col = lax.broadcasted_iota(jnp.int32, (t, t), 1)
    below = row > col
    tri = below.astype(BF16)
    streams = [(s, g) for s in range(nq) for g in range(heads)]
    rows = [slice(s * t, (s + 1) * t) for s in range(nq)]
    cols = [slice(g * HEAD_DIM, (g + 1) * HEAD_DIM) for g in range(heads)]

    def key_start(j, s):
        return pl.multiple_of(jnp.maximum(j + s, 0) * t, t)

    def scores(j):
        for u, (s, g) in enumerate(streams):
            z_ref[u] = lax.dot_general(
                q_ref[0, rows[s], cols[g]], k_ref[0, pl.ds(key_start(j, s), t), cols[g]],
                (((1,), (1,)), ((), ())), preferred_element_type=F32)

    def weighted_values(j):
        for u, (s, g) in enumerate(streams):
            acc_ref[rows[s], cols[g]] += jnp.dot(
                p_ref[u], v_ref[0, pl.ds(key_start(j, s), t), cols[g]],
                preferred_element_type=F32)

    def weights(j, on_diagonal):
        log_betas, log_1mbs, parts = [], [], []
        for u in range(len(streams)):
            z = z_ref[u] * scale
            log_beta = jnp.minimum(z, 0.0) - jnp.log(1.0 + jnp.exp(-jnp.abs(z)))
            log_1mb = log_beta - z
            if on_diagonal:
                log_1mb = jnp.where(below, log_1mb, 0.0)
            hi = log_1mb.astype(BF16)
            lo = (log_1mb - hi.astype(F32)).astype(BF16)
            log_betas.append(log_beta)
            log_1mbs.append(log_1mb)
            parts += [hi, lo]
        sums = jnp.dot(jnp.concatenate(parts, axis=0), tri, preferred_element_type=F32)
        scores(j - 1)
        r_max = None
        for u, (s, g) in enumerate(streams):
            suffix = sums[2 * u * t:(2 * u + 1) * t] + sums[(2 * u + 1) * t:(2 * u + 2) * t]
            r_old = jnp.where(j + s >= 0, r_ref[u], OUT_OF_KEYS)
            attn = jnp.exp(log_betas[u] + suffix + r_old)
            if on_diagonal:
                attn = jnp.where(below, attn, 0.0)
            p_ref[u] = attn.astype(BF16)
            r_new = r_old + suffix[:, 0:1] + log_1mbs[u][:, 0:1]
            r_ref[u] = r_new
            r_max = r_new if r_max is None else jnp.maximum(r_max, r_new)
        return jnp.max((r_max >= F32_EXP_ZERO_BELOW).astype(jnp.int32))

    acc_ref[...] = jnp.zeros_like(acc_ref)
    r_ref[...] = jnp.zeros_like(r_ref)
    j0 = i * nq
    scores(j0)
    live0 = weights(j0, True)

    def cond(c):
        j, live = c
        return jnp.logical_and(j + (nq - 1) >= 0, live > 0)

    def body(c):
        j, _ = c
        weighted_values(j + 1)
        return j - 1, weights(j, False)

    j_end, _ = lax.while_loop(cond, body, (j0 - 1, live0))
    weighted_values(j_end + 1)
    mix = yag_ref[0].astype(F32) + sgb_ref[0].astype(F32) * acc_ref[...]
    o_ref[0] = mix.astype(BF16)


def _stickbreak(q, k, v, yag, sgb):
    b, s, d = q.shape
    t = T_ATT
    nq = ATT_Q_BLOCKS_PER_STEP
    hg = ATT_HEADS_PER_STEP
    kern = functools.partial(_attn_kernel, t=t, nq=nq, heads=hg,
                             scale=1.0 / math.sqrt(HEAD_DIM))
    q_spec = pl.BlockSpec((1, nq * t, hg * HEAD_DIM), lambda bi, hi, qi: (bi, qi, hi))
    kv_spec = pl.BlockSpec((1, s, hg * HEAD_DIM), lambda bi, hi, qi: (bi, 0, hi),
                           pipeline_mode=pl.Buffered(1))
    return pl.pallas_call(
        kern,
        grid=(b, ATT_HEADS // hg, s // (nq * t)),
        in_specs=[q_spec, kv_spec, kv_spec, q_spec, q_spec],
        out_specs=q_spec,
        out_shape=jax.ShapeDtypeStruct((b, s, d), BF16),
        scratch_shapes=[
            pltpu.VMEM((nq * t, hg * HEAD_DIM), F32),
            pltpu.VMEM((nq * hg, t, 1), F32),
            pltpu.VMEM((nq * hg, t, t), F32),
            pltpu.VMEM((nq * hg, t, t), BF16),
        ],
        compiler_params=pltpu.CompilerParams(
            dimension_semantics=("arbitrary", "arbitrary", "arbitrary"),
            vmem_limit_bytes=_vmem_limit(60 * 1024 * 1024)),
        name="stickbreak",
    )(q, k, v, yag, sgb)


def _mix_ffn_kernel(x_ref, mix_ref, mod_ref, wo_ref, g2_ref, wu_ref,
                    cw_ref, cb_ref, wd_ref, o_ref, ext_ref, hist_ref, act_ref, *, tm, d, dff):
    t_idx = pl.program_id(1)
    halo = V7X_SUBLANES

    @pl.when(t_idx == 0)
    def _():
        hist_ref[...] = jnp.zeros_like(hist_ref)

    gt1 = mod_ref[0, 2:3, :]
    sh2 = mod_ref[0, 3:4, :]
    sc2 = mod_ref[0, 4:5, :]
    gt2 = mod_ref[0, 5:6, :]

    x1 = x_ref[0] + gt1 * jnp.dot(mix_ref[0], wo_ref[...], preferred_element_type=F32)
    h2 = ((x1 * _rms_scale(x1) * g2_ref[...]) * (1.0 + sc2) + sh2).astype(BF16)

    def col_slices(c):
        return (slice(c * FF_CHUNK, (c + 1) * FF_CHUNK),
                slice(dff + c * FF_CHUNK, dff + (c + 1) * FF_CHUNK))

    def up_proj(c):
        for part, cs in enumerate(col_slices(c)):
            buf = 2 * (c % 2) + part
            ext_ref[buf, 0:halo, :] = hist_ref[:, cs]
            ext_ref[buf, halo:halo + tm, :] = jnp.dot(h2, wu_ref[:, cs],
                                                      preferred_element_type=F32)
            hist_ref[:, cs] = ext_ref[buf, tm:tm + halo, :]

    def conv(c, part):
        cs = col_slices(c)[part]
        buf = 2 * (c % 2) + part
        out = cb_ref[:, cs] + cw_ref[FFN_CONV - 1:FFN_CONV, cs] * ext_ref[buf, halo:halo + tm, :]
        for kk in range(FFN_CONV - 1):
            shift = FFN_CONV - 1 - kk
            out = out + cw_ref[kk:kk + 1, cs] * ext_ref[buf, halo - shift:halo - shift + tm, :]
        return out

    def down_proj(c):
        return jnp.dot(act_ref[c % 2], wd_ref[c * FF_CHUNK:(c + 1) * FF_CHUNK, :],
                       preferred_element_type=F32)

    n_chunks = dff // FF_CHUNK
    acc = jnp.zeros((tm, d), F32)
    up_proj(0)
    for c in range(n_chunks):
        if c + 1 < n_chunks:
            up_proj(c + 1)
        if c > 0:
            acc = acc + down_proj(c - 1)
        act_ref[c % 2] = (_gelu_tanh(conv(c, 0)) * conv(c, 1)).astype(BF16)
    acc = acc + down_proj(n_chunks - 1)
    o_ref[0] = x1 + gt2 * acc


def _mix_ffn(x, mix, mod, w_out, g2, w_up, conv_w, conv_b, w_down):
    b, s, d = x.shape
    dff = w_down.shape[0]
    tm = TM_FFN
    tok_spec = pl.BlockSpec((1, tm, d), lambda bi, ti: (bi, ti, 0))
    kern = functools.partial(_mix_ffn_kernel, tm=tm, d=d, dff=dff)
    return pl.pallas_call(
        kern,
        grid=(b, s // tm),
        in_specs=[
            tok_spec, tok_spec,
            pl.BlockSpec((1, 6, d), lambda bi, ti: (bi, 0, 0)),
            _const_spec(w_out.shape),
            _const_spec((1, d)),
            _const_spec(w_up.shape),
            _const_spec(conv_w.shape),
            _const_spec((1, 2 * dff)),
            _const_spec(w_down.shape),
        ],
        out_specs=tok_spec,
        out_shape=jax.ShapeDtypeStruct((b, s, d), F32),
        scratch_shapes=[
            pltpu.VMEM((4, tm + V7X_SUBLANES, FF_CHUNK), F32),
            pltpu.VMEM((V7X_SUBLANES, 2 * dff), F32),
            pltpu.VMEM((2, tm, FF_CHUNK), BF16),
        ],
        compiler_params=pltpu.CompilerParams(
            dimension_semantics=("arbitrary", "arbitrary"),
            vmem_limit_bytes=_vmem_limit(60 * 1024 * 1024)),
        name="mix_ffn",
    )(x, mix, mod, w_out, g2, w_up, conv_w, conv_b, w_down)


def kernel(x, c, ada_w, ada_b, norm1_g, w_in, conv_w, conv_b, rg_wa, rg_ba, rg_wx, rg_bx,
           rg_lambda, q_norm_g, k_norm_g, w_out, norm2_g, ffn_up, ffn_conv_w, ffn_conv_b,
           ffn_down):
    depth, d, _ = ada_w.shape
    b = x.shape[0]
    for tile in (TM_IN, TM_FFN, T_ATT * ATT_Q_BLOCKS_PER_STEP):
        assert x.shape[1] % tile == 0
    assert d == ATT_HEADS * HEAD_DIM

    mod_all = _adaln_mod(c, ada_w, ada_b).reshape(depth, b, 6, d)
    for l in range(depth):
        mod = mod_all[l]
        w_gate = jnp.concatenate([rg_wa[l], rg_wx[l]], axis=-1).astype(BF16)
        q, k, v, yag, sgb = _in_proj_rglru(
            x, mod, norm1_g[l][None], w_in[l].astype(BF16), conv_w[l], conv_b[l][None],
            w_gate, rg_ba[l][None], rg_bx[l][None], rg_lambda[l][None],
            q_norm_g[l][None], k_norm_g[l][None])
        mix = _stickbreak(q, k, v, yag, sgb)
        x = _mix_ffn(x, mix, mod, w_out[l].astype(BF16), norm2_g[l][None],
                     ffn_up[l].astype(BF16), ffn_conv_w[l], ffn_conv_b[l][None],
                     ffn_down[l].astype(BF16))
    return x
```

```python
import functools
import math

import jax
import jax.numpy as jnp
from jax import lax
from jax.experimental import pallas as pl
from jax.experimental.pallas import tpu as pltpu

F32 = jnp.float32
BF16 = jnp.bfloat16

V7X_LANES = 128
V7X_SUBLANES = 8
V7X_VMEM_BYTES = 64 * 1024 * 1024

RNN_HEADS = 8
RNN_BLOCK = 128
RNN_CONV = 4
RG_C = 8.0
ATT_HEADS = 8
HEAD_DIM = 128
FFN_CONV = 3
EPS = 1e-6

F32_EXP_ZERO_BELOW = -105.0
OUT_OF_KEYS = -1e30

TM_IN = 512
TM_FFN = 512
T_ATT = 128
ATT_Q_BLOCKS_PER_STEP = 4
ATT_HEADS_PER_STEP = 8
FF_CHUNK = 512
MOD_TN = 1536


def _vmem_limit(nbytes):
    return int(min(nbytes, V7X_VMEM_BYTES - 4 * 1024 * 1024))


def _const_spec(shape):
    zeros = (0,) * len(shape)
    return pl.BlockSpec(shape, lambda *_: zeros, pipeline_mode=pl.Buffered(1))


def _gelu_tanh(x):
    c = math.sqrt(2.0 / math.pi)
    return 0.5 * x * (1.0 + jnp.tanh(c * (x + 0.044715 * (x * x * x))))


def _sigmoid(x):
    return 1.0 / (1.0 + jnp.exp(-x))


def _rms_scale(x):
    return lax.rsqrt(jnp.mean(x * x, axis=-1, keepdims=True) + EPS)


def _mod_kernel(c_ref, w_ref, b_ref, o_ref):
    acc = jnp.dot(c_ref[...].astype(BF16), w_ref[0].astype(BF16),
                  preferred_element_type=F32)
    o_ref[0] = acc + b_ref[0]


def _adaln_mod(c, ada_w, ada_b):
    depth, d, n = ada_w.shape
    b = c.shape[0]
    return pl.pallas_call(
        _mod_kernel,
        grid=(depth, n // MOD_TN),
        in_specs=[
            pl.BlockSpec((b, d), lambda l, j: (0, 0)),
            pl.BlockSpec((1, d, MOD_TN), lambda l, j: (l, 0, j)),
            pl.BlockSpec((1, 1, MOD_TN), lambda l, j: (l, 0, j)),
        ],
        out_specs=pl.BlockSpec((1, b, MOD_TN), lambda l, j: (l, 0, j)),
        out_shape=jax.ShapeDtypeStruct((depth, b, n), F32),
        compiler_params=pltpu.CompilerParams(
            dimension_semantics=("arbitrary", "arbitrary"),
            vmem_limit_bytes=_vmem_limit(32 * 1024 * 1024)),
        name="adaln_mod",
    )(c, ada_w, ada_b.reshape(depth, 1, n))


def _in_proj_kernel(x_ref, mod_ref, g1_ref, w_ref, cw_ref, cb_ref, wg_ref, ba_ref, bx_ref,
                    lam_ref, qg_ref, kg_ref,
                    q_ref, k_ref, v_ref, yag_ref, sgb_ref,
                    ext_ref, hs_ref, h_ref, *, tm, d):
    t_idx = pl.program_id(1)
    halo = V7X_SUBLANES

    @pl.when(t_idx == 0)
    def _():
        ext_ref[0:halo, :] = jnp.zeros((halo, d), F32)
        h_ref[...] = jnp.zeros((1, d), F32)

    x = x_ref[0]
    sh1 = mod_ref[0, 0:1, :]
    sc1 = mod_ref[0, 1:2, :]
    h = (x * _rms_scale(x) * g1_ref[...]) * (1.0 + sc1) + sh1
    hb = h.astype(BF16)

    half = d // 2

    def proj(col, part=None):
        c0 = col * d if part is None else col * d + part * half
        c1 = (col + 1) * d if part is None else c0 + half
        return jnp.dot(hb, w_ref[:, c0:c1], preferred_element_type=F32)

    def head_norm(p, gain, o_ref, first_head):
        for hh in range(p.shape[1] // HEAD_DIM):
            ph = p[:, hh * HEAD_DIM:(hh + 1) * HEAD_DIM]
            c0 = (first_head + hh) * HEAD_DIM
            o_ref[0, :, c0:c0 + HEAD_DIM] = (ph * _rms_scale(ph) * gain).astype(BF16)

    ext_ref[halo:halo + tm, :] = proj(0)
    q_proj = proj(2)
    xc = cb_ref[...] + cw_ref[RNN_CONV - 1:RNN_CONV, :] * ext_ref[halo:halo + tm, :]
    for kk in range(RNN_CONV - 1):
        shift = RNN_CONV - 1 - kk
        xc = xc + cw_ref[kk:kk + 1, :] * ext_ref[halo - shift:halo - shift + tm, :]
    ext_ref[0:halo, :] = ext_ref[tm:tm + halo, :]
    head_norm(q_proj, qg_ref[...] * (1.0 / math.sqrt(HEAD_DIM)), q_ref, 0)

    pieces = [(3, 0), (3, 1), (4, 0), (4, 1), (6, 0), (6, 1), (1, 0), (5, 0), (1, 1), (5, 1)]
    y_half = {}

    def finish(piece, val):
        col, part = piece
        cs = slice(part * half, (part + 1) * half)
        if col == 3:
            head_norm(val, kg_ref[...], k_ref, part * (ATT_HEADS // 2))
        elif col == 4:
            v_ref[0, :, cs] = val.astype(BF16)
        elif col == 6:
            sgb_ref[0, :, cs] = _sigmoid(val).astype(BF16)
        elif col == 1:
            y_half[part] = _gelu_tanh(val)
        else:
            ext_ref[halo:halo + tm, cs] = y_half[part] * _sigmoid(val)

    row = lax.broadcasted_iota(jnp.int32, (V7X_SUBLANES, RNN_BLOCK), 0)

    def scan_head(a_all, u_all, sl):
        carry = h_ref[:, sl]
        for g in range(tm // V7X_SUBLANES):
            rows = slice(g * V7X_SUBLANES, (g + 1) * V7X_SUBLANES)
            a, u = a_all[rows], u_all[rows]
            for dist in (1, 2, 4):
                keep = row >= dist
                a_prev = pltpu.roll(a, dist, axis=0)
                u_prev = pltpu.roll(u, dist, axis=0)
                u = jnp.where(keep, a * u_prev + u, u)
                a = jnp.where(keep, a * a_prev, a)
            hs = a * carry + u
            hs_ref[rows, sl] = hs
            carry = hs[V7X_SUBLANES - 1:V7X_SUBLANES, :]
        h_ref[:, sl] = carry

    decay = -RG_C * jnp.log(1.0 + jnp.exp(-lam_ref[...]))
    pending = None
    for hh in range(RNN_HEADS):
        sl = slice(hh * RNN_BLOCK, (hh + 1) * RNN_BLOCK)
        xh = xc[:, sl]
        ri = jnp.dot(xh.astype(BF16), wg_ref[hh], preferred_element_type=F32)
        current = (pieces[hh], proj(*pieces[hh]))
        r = _sigmoid(ri[:, :RNN_BLOCK] + ba_ref[:, sl])
        ig = _sigmoid(ri[:, RNN_BLOCK:] + bx_ref[:, sl])
        a = jnp.exp(r * decay[:, sl])
        scan_head(a, jnp.sqrt(1.0 - a * a) * (ig * xh), sl)
        if pending is not None:
            finish(*pending)
        pending = current
    for piece in pieces[RNN_HEADS:]:
        current = (piece, proj(*piece))
        finish(*pending)
        pending = current
    finish(*pending)

    yag_ref[0] = (hs_ref[...] * ext_ref[halo:halo + tm, :]).astype(BF16)


def _in_proj_rglru(x, mod, g1, w_in, conv_w, conv_b, w_gate, ba, bx, lam, qg, kg):
    b, s, d = x.shape
    tm = TM_IN
    tok_spec = pl.BlockSpec((1, tm, d), lambda bi, ti: (bi, ti, 0))
    act = jax.ShapeDtypeStruct((b, s, d), BF16)
    kern = functools.partial(_in_proj_kernel, tm=tm, d=d)
    return pl.pallas_call(
        kern,
        grid=(b, s // tm),
        in_specs=[
            tok_spec,
            pl.BlockSpec((1, 6, d), lambda bi, ti: (bi, 0, 0)),
            _const_spec((1, d)),
            _const_spec(w_in.shape),
            _const_spec(conv_w.shape),
            _const_spec((1, d)),
            _const_spec(w_gate.shape),
            _const_spec((1, d)),
            _const_spec((1, d)),
            _const_spec((1, d)),
            _const_spec((1, HEAD_DIM)),
            _const_spec((1, HEAD_DIM)),
        ],
        out_specs=[tok_spec] * 5,
        out_shape=[act] * 5,
        scratch_shapes=[
            pltpu.VMEM((tm + V7X_SUBLANES, d), F32),
            pltpu.VMEM((tm, d), F32),
            pltpu.VMEM((1, d), F32),
        ],
        compiler_params=pltpu.CompilerParams(
            dimension_semantics=("arbitrary", "arbitrary"),
            vmem_limit_bytes=_vmem_limit(60 * 1024 * 1024)),
        name="in_proj_rglru",
    )(x, mod, g1, w_in, conv_w, conv_b, w_gate, ba, bx, lam, qg, kg)


def _attn_kernel(q_ref, k_ref, v_ref, yag_ref, sgb_ref, o_ref, acc_ref, r_ref, z_ref, p_ref,
                 *, t, nq, heads):
    i = pl.program_id(2)
    row = lax.broadcasted_iota(jnp.int32, (t, t), 0)
    col = lax.broadcasted_iota(jnp.int32, (t, t), 1)
    below = row > col
    tri = below.astype(BF16)
    streams = [(s, g) for s in range(nq) for g in range(heads)]
    rows = [slice(s * t, (s + 1) * t) for s in range(nq)]
    cols = [slice(g * HEAD_DIM, (g + 1) * HEAD_DIM) for g in range(heads)]

    def key_start(j, s):
        return pl.multiple_of(jnp.maximum(j + s, 0) * t, t)

    def scores(j):
        for u, (s, g) in enumerate(streams):
            z_ref[u] = lax.dot_general(
                q_ref[0, rows[s], cols[g]], k_ref[0, pl.ds(key_start(j, s), t), cols[g]],
                (((1,), (1,)), ((), ())), preferred_element_type=F32)

    def weighted_values(j):
        for u, (s, g) in enumerate(streams):
            acc_ref[rows[s], cols[g]] += jnp.dot(
                p_ref[u], v_ref[0, pl.ds(key_start(j, s), t), cols[g]],
                preferred_element_type=F32)

    def weights(j, on_diagonal):
        log_betas, log_1mbs, parts = [], [], []
        for u in range(len(streams)):
            z = z_ref[u]
            log_beta = jnp.minimum(z, 0.0) - jnp.log(1.0 + jnp.exp(-jnp.abs(z)))
            log_1mb = log_beta - z
            if on_diagonal:
                log_1mb = jnp.where(below, log_1mb, 0.0)
            log_betas.append(log_beta)
            log_1mbs.append(log_1mb)
            parts.append(log_1mb.astype(BF16))
        sums = jnp.dot(jnp.concatenate(parts, axis=0), tri, preferred_element_type=F32)
        scores(j - 1)
        r_max = None
        for u, (s, g) in enumerate(streams):
            suffix = sums[u * t:(u + 1) * t]
            r_old = jnp.where(j + s >= 0, r_ref[u], OUT_OF_KEYS)
            attn = jnp.exp(log_betas[u] + suffix + r_old)
            if on_diagonal:
                attn = jnp.where(below, attn, 0.0)
            p_ref[u] = attn.astype(BF16)
            r_new = r_old + suffix[:, 0:1] + log_1mbs[u][:, 0:1]
            r_ref[u] = r_new
            r_max = r_new if r_max is None else jnp.maximum(r_max, r_new)
        return jnp.max((r_max >= F32_EXP_ZERO_BELOW).astype(jnp.int32))

    acc_ref[...] = jnp.zeros_like(acc_ref)
    r_ref[...] = jnp.zeros_like(r_ref)
    j0 = i * nq
    scores(j0)
    live0 = weights(j0, True)

    def cond(c):
        j, live = c
        return jnp.logical_and(j + (nq - 1) >= 0, live > 0)

    def body(c):
        j, _ = c
        weighted_values(j + 1)
        return j - 1, weights(j, False)

    j_end, _ = lax.while_loop(cond, body, (j0 - 1, live0))
    weighted_values(j_end + 1)
    mix = yag_ref[0].astype(F32) + sgb_ref[0].astype(F32) * acc_ref[...]
    o_ref[0] = mix.astype(BF16)


def _stickbreak(q, k, v, yag, sgb):
    b, s, d = q.shape
    t = T_ATT
    nq = ATT_Q_BLOCKS_PER_STEP
    hg = ATT_HEADS_PER_STEP
    kern = functools.partial(_attn_kernel, t=t, nq=nq, heads=hg)
    q_spec = pl.BlockSpec((1, nq * t, hg * HEAD_DIM), lambda bi, hi, qi: (bi, qi, hi))
    kv_spec = pl.BlockSpec((1, s, hg * HEAD_DIM), lambda bi, hi, qi: (bi, 0, hi),
                           pipeline_mode=pl.Buffered(1))
    return pl.pallas_call(
        kern,
        grid=(b, ATT_HEADS // hg, s // (nq * t)),
        in_specs=[q_spec, kv_spec, kv_spec, q_spec, q_spec],
        out_specs=q_spec,
        out_shape=jax.ShapeDtypeStruct((b, s, d), BF16),
        scratch_shapes=[
            pltpu.VMEM((nq * t, hg * HEAD_DIM), F32),
            pltpu.VMEM((nq * hg, t, 1), F32),
            pltpu.VMEM((nq * hg, t, t), F32),
            pltpu.VMEM((nq * hg, t, t), BF16),
        ],
        compiler_params=pltpu.CompilerParams(
            dimension_semantics=("arbitrary", "arbitrary", "arbitrary"),
            vmem_limit_bytes=_vmem_limit(60 * 1024 * 1024)),
        name="stickbreak",
    )(q, k, v, yag, sgb)


def _mix_ffn_kernel(x_ref, mix_ref, mod_ref, wo_ref, g2_ref, wu_ref,
                    cw_ref, cb_ref, wd_ref, o_ref, ext_ref, hist_ref, act_ref, *, tm, d, dff):
    t_idx = pl.program_id(1)
    halo = V7X_SUBLANES

    @pl.when(t_idx == 0)
    def _():
        hist_ref[...] = jnp.zeros_like(hist_ref)

    gt1 = mod_ref[0, 2:3, :]
    sh2 = mod_ref[0, 3:4, :]
    sc2 = mod_ref[0, 4:5, :]
    gt2 = mod_ref[0, 5:6, :]

    x1 = x_ref[0] + gt1 * jnp.dot(mix_ref[0], wo_ref[...], preferred_element_type=F32)
    h2 = ((x1 * _rms_scale(x1) * g2_ref[...]) * (1.0 + sc2) + sh2).astype(BF16)

    def col_slices(c):
        return (slice(c * FF_CHUNK, (c + 1) * FF_CHUNK),
                slice(dff + c * FF_CHUNK, dff + (c + 1) * FF_CHUNK))

    def up_proj(c):
        for part, cs in enumerate(col_slices(c)):
            buf = 2 * (c % 2) + part
            ext_ref[buf, 0:halo, :] = hist_ref[:, cs]
            ext_ref[buf, halo:halo + tm, :] = jnp.dot(h2, wu_ref[:, cs],
                                                      preferred_element_type=F32)
            hist_ref[:, cs] = ext_ref[buf, tm:tm + halo, :]

    def conv(c, part):
        cs = col_slices(c)[part]
        buf = 2 * (c % 2) + part
        out = cb_ref[:, cs] + cw_ref[FFN_CONV - 1:FFN_CONV, cs] * ext_ref[buf, halo:halo + tm, :]
        for kk in range(FFN_CONV - 1):
            shift = FFN_CONV - 1 - kk
            out = out + cw_ref[kk:kk + 1, cs] * ext_ref[buf, halo - shift:halo - shift + tm, :]
        return out

    def down_proj(c):
        return jnp.dot(act_ref[c % 2], wd_ref[c * FF_CHUNK:(c + 1) * FF_CHUNK, :],
                       preferred_element_type=F32)

    n_chunks = dff // FF_CHUNK
    acc = jnp.zeros((tm, d), F32)
    up_proj(0)
    for c in range(n_chunks):
        if c + 1 < n_chunks:
            up_proj(c + 1)
        if c > 0:
            acc = acc + down_proj(c - 1)
        act_ref[c % 2] = (_gelu_tanh(conv(c, 0)) * conv(c, 1)).astype(BF16)
    acc = acc + down_proj(n_chunks - 1)
    o_ref[0] = x1 + gt2 * acc


def _mix_ffn(x, mix, mod, w_out, g2, w_up, conv_w, conv_b, w_down):
    b, s, d = x.shape
    dff = w_down.shape[0]
    tm = TM_FFN
    tok_spec = pl.BlockSpec((1, tm, d), lambda bi, ti: (bi, ti, 0))
    kern = functools.partial(_mix_ffn_kernel, tm=tm, d=d, dff=dff)
    return pl.pallas_call(
        kern,
        grid=(b, s // tm),
        in_specs=[
            tok_spec, tok_spec,
            pl.BlockSpec((1, 6, d), lambda bi, ti: (bi, 0, 0)),
            _const_spec(w_out.shape),
            _const_spec((1, d)),
            _const_spec(w_up.shape),
            _const_spec(conv_w.shape),
            _const_spec((1, 2 * dff)),
            _const_spec(w_down.shape),
        ],
        out_specs=tok_spec,
        out_shape=jax.ShapeDtypeStruct((b, s, d), F32),
        scratch_shapes=[
            pltpu.VMEM((4, tm + V7X_SUBLANES, FF_CHUNK), F32),
            pltpu.VMEM((V7X_SUBLANES, 2 * dff), F32),
            pltpu.VMEM((2, tm, FF_CHUNK), BF16),
        ],
        compiler_params=pltpu.CompilerParams(
            dimension_semantics=("arbitrary", "arbitrary"),
            vmem_limit_bytes=_vmem_limit(60 * 1024 * 1024)),
        name="mix_ffn",
    )(x, mix, mod, w_out, g2, w_up, conv_w, conv_b, w_down)


def kernel(x, c, ada_w, ada_b, norm1_g, w_in, conv_w, conv_b, rg_wa, rg_ba, rg_wx, rg_bx,
           rg_lambda, q_norm_g, k_norm_g, w_out, norm2_g, ffn_up, ffn_conv_w, ffn_conv_b,
           ffn_down):
    depth, d, _ = ada_w.shape
    b = x.shape[0]
    for tile in (TM_IN, TM_FFN, T_ATT * ATT_Q_BLOCKS_PER_STEP):
        assert x.shape[1] % tile == 0
    assert d == ATT_HEADS * HEAD_DIM

    mod_all = _adaln_mod(c, ada_w, ada_b).reshape(depth, b, 6, d)
    for l in range(depth):
        mod = mod_all[l]
        w_gate = jnp.concatenate([rg_wa[l], rg_wx[l]], axis=-1).astype(BF16)
        q, k, v, yag, sgb = _in_proj_rglru(
            x, mod, norm1_g[l][None], w_in[l].astype(BF16), conv_w[l], conv_b[l][None],
            w_gate, rg_ba[l][None], rg_bx[l][None], rg_lambda[l][None],
            q_norm_g[l][None], k_norm_g[l][None])
        mix = _stickbreak(q, k, v, yag, sgb)
        x = _mix_ffn(x, mix, mod, w_out[l].astype(BF16), norm2_g[l][None],
                     ffn_up[l].astype(BF16), ffn_conv_w[l], ffn_conv_b[l][None],
                     ffn_down[l].astype(BF16))
    return x
```

```python
import functools
import math

import jax
import jax.numpy as jnp
from jax import lax
from jax.experimental import pallas as pl
from jax.experimental.pallas import tpu as pltpu

F32 = jnp.float32
BF16 = jnp.bfloat16

V7X_LANES = 128
V7X_SUBLANES = 8
V7X_VMEM_BYTES = 64 * 1024 * 1024

RNN_HEADS = 8
RNN_BLOCK = 128
RNN_CONV = 4
RG_C = 8.0
ATT_HEADS = 8
HEAD_DIM = 128
FFN_CONV = 3
EPS = 1e-6

F32_EXP_ZERO_BELOW = -105.0
OUT_OF_KEYS = -1e30

TM_IN = 512
TM_FFN = 512
T_ATT = 128
ATT_Q_BLOCKS_PER_STEP = 4
ATT_HEADS_PER_STEP = 8
FF_CHUNK = 512
MOD_TN = 1536


def _vmem_limit(nbytes):
    return int(min(nbytes, V7X_VMEM_BYTES - 4 * 1024 * 1024))


def _const_spec(shape):
    zeros = (0,) * len(shape)
    return pl.BlockSpec(shape, lambda *_: zeros, pipeline_mode=pl.Buffered(1))


def _gelu_tanh(x):
    c = math.sqrt(2.0 / math.pi)
    return 0.5 * x * (1.0 + jnp.tanh(c * (x + 0.044715 * (x * x * x))))


def _sigmoid(x):
    return 1.0 / (1.0 + jnp.exp(-x))


def _rms_scale(x):
    return lax.rsqrt(jnp.mean(x * x, axis=-1, keepdims=True) + EPS)


def _mod_kernel(c_ref, w_ref, b_ref, o_ref):
    acc = jnp.dot(c_ref[...].astype(BF16), w_ref[0].astype(BF16),
                  preferred_element_type=F32)
    o_ref[0] = acc + b_ref[0]


def _adaln_mod(c, ada_w, ada_b):
    depth, d, n = ada_w.shape
    b = c.shape[0]
    return pl.pallas_call(
        _mod_kernel,
        grid=(depth, n // MOD_TN),
        in_specs=[
            pl.BlockSpec((b, d), lambda l, j: (0, 0)),
            pl.BlockSpec((1, d, MOD_TN), lambda l, j: (l, 0, j)),
            pl.BlockSpec((1, 1, MOD_TN), lambda l, j: (l, 0, j)),
        ],
        out_specs=pl.BlockSpec((1, b, MOD_TN), lambda l, j: (l, 0, j)),
        out_shape=jax.ShapeDtypeStruct((depth, b, n), F32),
        compiler_params=pltpu.CompilerParams(
            dimension_semantics=("arbitrary", "arbitrary"),
            vmem_limit_bytes=_vmem_limit(32 * 1024 * 1024)),
        name="adaln_mod",
    )(c, ada_w, ada_b.reshape(depth, 1, n))


def _in_proj_kernel(x_ref, mod_ref, g1_ref, w_ref, cw_ref, cb_ref, wg_ref, ba_ref, bx_ref,
                    lam_ref, qg_ref, kg_ref,
                    q_ref, k_ref, v_ref, yag_ref, sgb_ref,
                    ext_ref, hs_ref, h_ref, *, tm, d):
    t_idx = pl.program_id(1)
    halo = V7X_SUBLANES

    @pl.when(t_idx == 0)
    def _():
        ext_ref[0:halo, :] = jnp.zeros((halo, d), F32)
        h_ref[...] = jnp.zeros((1, d), F32)

    x = x_ref[0]
    sh1 = mod_ref[0, 0:1, :]
    sc1 = mod_ref[0, 1:2, :]
    h = (x * _rms_scale(x) * g1_ref[...]) * (1.0 + sc1) + sh1
    hb = h.astype(BF16)

    half = d // 2

    def proj(col, part=None):
        c0 = col * d if part is None else col * d + part * half
        c1 = (col + 1) * d if part is None else c0 + half
        return jnp.dot(hb, w_ref[:, c0:c1], preferred_element_type=F32)

    def head_norm(p, gain, o_ref, first_head):
        for hh in range(p.shape[1] // HEAD_DIM):
            ph = p[:, hh * HEAD_DIM:(hh + 1) * HEAD_DIM]
            c0 = (first_head + hh) * HEAD_DIM
            o_ref[0, :, c0:c0 + HEAD_DIM] = (ph * _rms_scale(ph) * gain).astype(BF16)

    ext_ref[halo:halo + tm, :] = proj(0)
    q_proj = proj(2)
    xc = cb_ref[...] + cw_ref[RNN_CONV - 1:RNN_CONV, :] * ext_ref[halo:halo + tm, :]
    for kk in range(RNN_CONV - 1):
        shift = RNN_CONV - 1 - kk
        xc = xc + cw_ref[kk:kk + 1, :] * ext_ref[halo - shift:halo - shift + tm, :]
    ext_ref[0:halo, :] = ext_ref[tm:tm + halo, :]
    head_norm(q_proj, qg_ref[...] * (1.0 / math.sqrt(HEAD_DIM)), q_ref, 0)

    pieces = [(3, 0), (3, 1), (4, 0), (4, 1), (6, 0), (6, 1), (1, 0), (5, 0), (1, 1), (5, 1)]
    y_half = {}

    def finish(piece, val):
        col, part = piece
        cs = slice(part * half, (part + 1) * half)
        if col == 3:
            head_norm(val, kg_ref[...], k_ref, part * (ATT_HEADS // 2))
        elif col == 4:
            v_ref[0, :, cs] = val.astype(BF16)
        elif col == 6:
            sgb_ref[0, :, cs] = _sigmoid(val).astype(BF16)
        elif col == 1:
            y_half[part] = _gelu_tanh(val)
        else:
            ext_ref[halo:halo + tm, cs] = y_half[part] * _sigmoid(val)

    row = lax.broadcasted_iota(jnp.int32, (V7X_SUBLANES, RNN_BLOCK), 0)

    def scan_head(a_all, u_all, sl):
        carry = h_ref[:, sl]
        for g in range(tm // V7X_SUBLANES):
            rows = slice(g * V7X_SUBLANES, (g + 1) * V7X_SUBLANES)
            a, u = a_all[rows], u_all[rows]
            for dist in (1, 2, 4):
                keep = row >= dist
                a_prev = pltpu.roll(a, dist, axis=0)
                u_prev = pltpu.roll(u, dist, axis=0)
                u = jnp.where(keep, a * u_prev + u, u)
                a = jnp.where(keep, a * a_prev, a)
            hs = a * carry + u
            hs_ref[rows, sl] = hs
            carry = hs[V7X_SUBLANES - 1:V7X_SUBLANES, :]
        h_ref[:, sl] = carry

    decay = -RG_C * jnp.log(1.0 + jnp.exp(-lam_ref[...]))
    pending = None
    for hh in range(RNN_HEADS):
        sl = slice(hh * RNN_BLOCK, (hh + 1) * RNN_BLOCK)
        xh = xc[:, sl]
        ri = jnp.dot(xh.astype(BF16), wg_ref[hh], preferred_element_type=F32)
        current = (pieces[hh], proj(*pieces[hh]))
        r = _sigmoid(ri[:, :RNN_BLOCK] + ba_ref[:, sl])
        ig = _sigmoid(ri[:, RNN_BLOCK:] + bx_ref[:, sl])
        a = jnp.exp(r * decay[:, sl])
        scan_head(a, jnp.sqrt(1.0 - a * a) * (ig * xh), sl)
        if pending is not None:
            finish(*pending)
        pending = current
    for piece in pieces[RNN_HEADS:]:
        current = (piece, proj(*piece))
        finish(*pending)
        pending = current
    finish(*pending)

    yag_ref[0] = (hs_ref[...] * ext_ref[halo:halo + tm, :]).astype(BF16)


def _in_proj_rglru(x, mod, g1, w_in, conv_w, conv_b, w_gate, ba, bx, lam, qg, kg):
    b, s, d = x.shape
    tm = TM_IN
    tok_spec = pl.BlockSpec((1, tm, d), lambda bi, ti: (bi, ti, 0))
    act = jax.ShapeDtypeStruct((b, s, d), BF16)
    kern = functools.partial(_in_proj_kernel, tm=tm, d=d)
    return pl.pallas_call(
        kern,
        grid=(b, s // tm),
        in_specs=[
            tok_spec,
            pl.BlockSpec((1, 6, d), lambda bi, ti: (bi, 0, 0)),
            _const_spec((1, d)),
            _const_spec(w_in.shape),
            _const_spec(conv_w.shape),
            _const_spec((1, d)),
            _const_spec(w_gate.shape),
            _const_spec((1, d)),
            _const_spec((1, d)),
            _const_spec((1, d)),
            _const_spec((1, HEAD_DIM)),
            _const_spec((1, HEAD_DIM)),
        ],
        out_specs=[tok_spec] * 5,
        out_shape=[act] * 5,
        scratch_shapes=[
            pltpu.VMEM((tm + V7X_SUBLANES, d), F32),
            pltpu.VMEM((tm, d), F32),
            pltpu.VMEM((1, d), F32),
        ],
        compiler_params=pltpu.CompilerParams(
            dimension_semantics=("arbitrary", "arbitrary"),
            vmem_limit_bytes=_vmem_limit(60 * 1024 * 1024)),
        name="in_proj_rglru",
    )(x, mod, g1, w_in, conv_w, conv_b, w_gate, ba, bx, lam, qg, kg)


def _attn_kernel(q_ref, k_ref, v_ref, yag_ref, sgb_ref, sel_ref, o_ref,
                 acc_ref, r_ref, z_ref, p_ref, *, t, nq, heads):
    i = pl.program_id(2)
    row = lax.broadcasted_iota(jnp.int32, (t, t), 0)
    col = lax.broadcasted_iota(jnp.int32, (t, t), 1)
    below = row > col
    tri = below.astype(BF16)
    streams = [(s, g) for s in range(nq) for g in range(heads)]
    rows = [slice(s * t, (s + 1) * t) for s in range(nq)]
    cols = [slice(g * HEAD_DIM, (g + 1) * HEAD_DIM) for g in range(heads)]

    def key_start(j, s):
        return pl.multiple_of(jnp.maximum(j + s, 0) * t, t)

    def scores(j, group=None):
        for u in (range(len(streams)) if group is None else group):
            s, g = streams[u]
            z_ref[u] = lax.dot_general(
                q_ref[0, rows[s], cols[g]], k_ref[0, pl.ds(key_start(j, s), t), cols[g]],
                (((1,), (1,)), ((), ())), preferred_element_type=F32)

    def weighted_values(j):
        for u, (s, g) in enumerate(streams):
            acc_ref[rows[s], cols[g]] += jnp.dot(
                p_ref[u], v_ref[0, pl.ds(key_start(j, s), t), cols[g]],
                preferred_element_type=F32)

    def weights(j, on_diagonal):
        log_betas, sums, totals = {}, {}, None
        n_groups = 4
        per_group = len(streams) // n_groups
        for gi in range(n_groups):
            group = range(gi * per_group, (gi + 1) * per_group)
            parts = []
            for u in group:
                z = z_ref[u]
                log_beta = jnp.minimum(z, 0.0) - jnp.log(1.0 + jnp.exp(-jnp.abs(z)))
                log_1mb = log_beta - z
                if on_diagonal:
                    log_1mb = jnp.where(below, log_1mb, 0.0)
                log_betas[u] = log_beta
                parts.append(log_1mb.astype(BF16))
            group_sums = jnp.dot(jnp.concatenate(parts, axis=0), tri,
                                 preferred_element_type=F32)
            for n, u in enumerate(group):
                sums[u] = group_sums[n * t:(n + 1) * t]
            group_totals = jnp.dot(
                jnp.concatenate(parts, axis=1),
                sel_ref[gi * per_group * t:(gi + 1) * per_group * t, :],
                preferred_element_type=F32)
            totals = group_totals if totals is None else totals + group_totals
            scores(j - 1, group)
        r_old = jnp.where(lane >= -j * heads, r_ref[...], OUT_OF_KEYS)
        for u in range(len(streams)):
            attn = jnp.exp(log_betas[u] + sums[u] + r_old[:, u:u + 1])
            if on_diagonal:
                attn = jnp.where(below, attn, 0.0)
            p_ref[u] = attn.astype(BF16)
        r_new = r_old + totals
        r_ref[...] = r_new
        live = jnp.logical_and(lane < len(streams), r_new >= F32_EXP_ZERO_BELOW)
        return jnp.max(live.astype(jnp.int32))

    lane = lax.broadcasted_iota(jnp.int32, (t, V7X_LANES), 1)
    acc_ref[...] = jnp.zeros_like(acc_ref)
    r_ref[...] = jnp.zeros_like(r_ref)
    j0 = i * nq
    scores(j0)
    live0 = weights(j0, True)

    def cond(c):
        j, live = c
        return jnp.logical_and(j + (nq - 1) >= 0, live > 0)

    def body(c):
        j, _ = c
        weighted_values(j + 1)
        return j - 1, weights(j, False)

    j_end, _ = lax.while_loop(cond, body, (j0 - 1, live0))
    weighted_values(j_end + 1)
    mix = yag_ref[0].astype(F32) + sgb_ref[0].astype(F32) * acc_ref[...]
    o_ref[0] = mix.astype(BF16)


def _stickbreak(q, k, v, yag, sgb):
    b, s, d = q.shape
    t = T_ATT
    nq = ATT_Q_BLOCKS_PER_STEP
    hg = ATT_HEADS_PER_STEP
    kern = functools.partial(_attn_kernel, t=t, nq=nq, heads=hg)
    n_streams = nq * hg
    assert n_streams <= V7X_LANES
    sel = (jnp.arange(n_streams * t)[:, None] // t == jnp.arange(V7X_LANES)[None, :]).astype(BF16)
    q_spec = pl.BlockSpec((1, nq * t, hg * HEAD_DIM), lambda bi, hi, qi: (bi, qi, hi))
    kv_spec = pl.BlockSpec((1, s, hg * HEAD_DIM), lambda bi, hi, qi: (bi, 0, hi),
                           pipeline_mode=pl.Buffered(1))
    return pl.pallas_call(
        kern,
        grid=(b, ATT_HEADS // hg, s // (nq * t)),
        in_specs=[q_spec, kv_spec, kv_spec, q_spec, q_spec, _const_spec(sel.shape)],
        out_specs=q_spec,
        out_shape=jax.ShapeDtypeStruct((b, s, d), BF16),
        scratch_shapes=[
            pltpu.VMEM((nq * t, hg * HEAD_DIM), F32),
            pltpu.VMEM((t, V7X_LANES), F32),
            pltpu.VMEM((nq * hg, t, t), F32),
            pltpu.VMEM((nq * hg, t, t), BF16),
        ],
        compiler_params=pltpu.CompilerParams(
            dimension_semantics=("arbitrary", "arbitrary", "arbitrary"),
            vmem_limit_bytes=_vmem_limit(60 * 1024 * 1024)),
        name="stickbreak",
    )(q, k, v, yag, sgb, sel)


def _mix_ffn_kernel(x_ref, mix_ref, mod_ref, wo_ref, g2_ref, wu_ref,
                    cw_ref, cb_ref, wd_ref, o_ref, ext_ref, hist_ref, act_ref, *, tm, d, dff):
    t_idx = pl.program_id(1)
    halo = V7X_SUBLANES

    @pl.when(t_idx == 0)
    def _():
        hist_ref[...] = jnp.zeros_like(hist_ref)

    gt1 = mod_ref[0, 2:3, :]
    sh2 = mod_ref[0, 3:4, :]
    sc2 = mod_ref[0, 4:5, :]
    gt2 = mod_ref[0, 5:6, :]

    x1 = x_ref[0] + gt1 * jnp.dot(mix_ref[0], wo_ref[...], preferred_element_type=F32)
    h2 = ((x1 * _rms_scale(x1) * g2_ref[...]) * (1.0 + sc2) + sh2).astype(BF16)

    def col_slices(c):
        return (slice(c * FF_CHUNK, (c + 1) * FF_CHUNK),
                slice(dff + c * FF_CHUNK, dff + (c + 1) * FF_CHUNK))

    def up_proj(c):
        for part, cs in enumerate(col_slices(c)):
            buf = 2 * (c % 2) + part
            ext_ref[buf, 0:halo, :] = hist_ref[:, cs]
            ext_ref[buf, halo:halo + tm, :] = jnp.dot(h2, wu_ref[:, cs],
                                                      preferred_element_type=F32)
            hist_ref[:, cs] = ext_ref[buf, tm:tm + halo, :]

    def conv(c, part):
        cs = col_slices(c)[part]
        buf = 2 * (c % 2) + part
        out = cb_ref[:, cs] + cw_ref[FFN_CONV - 1:FFN_CONV, cs] * ext_ref[buf, halo:halo + tm, :]
        for kk in range(FFN_CONV - 1):
            shift = FFN_CONV - 1 - kk
            out = out + cw_ref[kk:kk + 1, cs] * ext_ref[buf, halo - shift:halo - shift + tm, :]
        return out

    def down_proj(c):
        return jnp.dot(act_ref[c % 2], wd_ref[c * FF_CHUNK:(c + 1) * FF_CHUNK, :],
                       preferred_element_type=F32)

    n_chunks = dff // FF_CHUNK
    acc = jnp.zeros((tm, d), F32)
    up_proj(0)
    for c in range(n_chunks):
        if c + 1 < n_chunks:
            up_proj(c + 1)
        if c > 0:
            acc = acc + down_proj(c - 1)
        act_ref[c % 2] = (_gelu_tanh(conv(c, 0)) * conv(c, 1)).astype(BF16)
    acc = acc + down_proj(n_chunks - 1)
    o_ref[0] = x1 + gt2 * acc


def _mix_ffn(x, mix, mod, w_out, g2, w_up, conv_w, conv_b, w_down):
    b, s, d = x.shape
    dff = w_down.shape[0]
    tm = TM_FFN
    tok_spec = pl.BlockSpec((1, tm, d), lambda bi, ti: (bi, ti, 0))
    kern = functools.partial(_mix_ffn_kernel, tm=tm, d=d, dff=dff)
    return pl.pallas_call(
        kern,
        grid=(b, s // tm),
        in_specs=[
            tok_spec, tok_spec,
            pl.BlockSpec((1, 6, d), lambda bi, ti: (bi, 0, 0)),
            _const_spec(w_out.shape),
            _const_spec((1, d)),
            _const_spec(w_up.shape),
            _const_spec(conv_w.shape),
            _const_spec((1, 2 * dff)),
            _const_spec(w_down.shape),
        ],
        out_specs=tok_spec,
        out_shape=jax.ShapeDtypeStruct((b, s, d), F32),
        scratch_shapes=[
            pltpu.VMEM((4, tm + V7X_SUBLANES, FF_CHUNK), F32),
            pltpu.VMEM((V7X_SUBLANES, 2 * dff), F32),
            pltpu.VMEM((2, tm, FF_CHUNK), BF16),
        ],
        compiler_params=pltpu.CompilerParams(
            dimension_semantics=("arbitrary", "arbitrary"),
            vmem_limit_bytes=_vmem_limit(60 * 1024 * 1024)),
        name="mix_ffn",
    )(x, mix, mod, w_out, g2, w_up, conv_w, conv_b, w_down)


def kernel(x, c, ada_w, ada_b, norm1_g, w_in, conv_w, conv_b, rg_wa, rg_ba, rg_wx, rg_bx,
           rg_lambda, q_norm_g, k_norm_g, w_out, norm2_g, ffn_up, ffn_conv_w, ffn_conv_b,
           ffn_down):
    depth, d, _ = ada_w.shape
    b = x.shape[0]
    for tile in (TM_IN, TM_FFN, T_ATT * ATT_Q_BLOCKS_PER_STEP):
        assert x.shape[1] % tile == 0
    assert d == ATT_HEADS * HEAD_DIM

    mod_all = _adaln_mod(c, ada_w, ada_b).reshape(depth, b, 6, d)
    for l in range(depth):
        mod = mod_all[l]
        w_gate = jnp.concatenate([rg_wa[l], rg_wx[l]], axis=-1).astype(BF16)
        q, k, v, yag, sgb = _in_proj_rglru(
            x, mod, norm1_g[l][None], w_in[l].astype(BF16), conv_w[l], conv_b[l][None],
            w_gate, rg_ba[l][None], rg_bx[l][None], rg_lambda[l][None],
            q_norm_g[l][None], k_norm_g[l][None])
        mix = _stickbreak(q, k, v, yag, sgb)
        x = _mix_ffn(x, mix, mod, w_out[l].astype(BF16), norm2_g[l][None],
                     ffn_up[l].astype(BF16), ffn_conv_w[l], ffn_conv_b[l][None],
                     ffn_down[l].astype(BF16))
    return x
```

```python
import functools
import math

import jax
import jax.numpy as jnp
from jax import lax
from jax.experimental import pallas as pl
from jax.experimental.pallas import tpu as pltpu

F32 = jnp.float32
BF16 = jnp.bfloat16

V7X_LANES = 128
V7X_SUBLANES = 8
V7X_VMEM_BYTES = 64 * 1024 * 1024

RNN_HEADS = 8
RNN_BLOCK = 128
RNN_CONV = 4
RG_C = 8.0
ATT_HEADS = 8
HEAD_DIM = 128
FFN_CONV = 3
EPS = 1e-6

F32_EXP_ZERO_BELOW = -105.0
OUT_OF_KEYS = -1e30

TM_IN = 512
TM_FFN = 512
T_ATT = 128
ATT_Q_BLOCKS_PER_STEP = 8
ATT_HEADS_PER_STEP = 4
FF_CHUNK = 512
MOD_TN = 1536


def _vmem_limit(nbytes):
    return int(min(nbytes, V7X_VMEM_BYTES - 4 * 1024 * 1024))


def _const_spec(shape):
    zeros = (0,) * len(shape)
    return pl.BlockSpec(shape, lambda *_: zeros, pipeline_mode=pl.Buffered(1))


def _gelu_tanh(x):
    c = math.sqrt(2.0 / math.pi)
    return 0.5 * x * (1.0 + jnp.tanh(c * (x + 0.044715 * (x * x * x))))


def _sigmoid(x):
    return 1.0 / (1.0 + jnp.exp(-x))


def _rms_scale(x):
    return lax.rsqrt(jnp.mean(x * x, axis=-1, keepdims=True) + EPS)


def _mod_kernel(c_ref, w_ref, b_ref, o_ref):
    acc = jnp.dot(c_ref[...].astype(BF16), w_ref[0].astype(BF16),
                  preferred_element_type=F32)
    o_ref[0] = acc + b_ref[0]


def _adaln_mod(c, ada_w, ada_b):
    depth, d, n = ada_w.shape
    b = c.shape[0]
    return pl.pallas_call(
        _mod_kernel,
        grid=(depth, n // MOD_TN),
        in_specs=[
            pl.BlockSpec((b, d), lambda l, j: (0, 0)),
            pl.BlockSpec((1, d, MOD_TN), lambda l, j: (l, 0, j)),
            pl.BlockSpec((1, 1, MOD_TN), lambda l, j: (l, 0, j)),
        ],
        out_specs=pl.BlockSpec((1, b, MOD_TN), lambda l, j: (l, 0, j)),
        out_shape=jax.ShapeDtypeStruct((depth, b, n), F32),
        compiler_params=pltpu.CompilerParams(
            dimension_semantics=("arbitrary", "arbitrary"),
            vmem_limit_bytes=_vmem_limit(32 * 1024 * 1024)),
        name="adaln_mod",
    )(c, ada_w, ada_b.reshape(depth, 1, n))


def _in_proj_kernel(x_ref, mod_ref, g1_ref, w_ref, cw_ref, cb_ref, wg_ref, ba_ref, bx_ref,
                    lam_ref, qg_ref, kg_ref,
                    q_ref, k_ref, v_ref, yag_ref, sgb_ref,
                    ext_ref, hs_ref, h_ref, *, tm, d):
    t_idx = pl.program_id(1)
    halo = V7X_SUBLANES

    @pl.when(t_idx == 0)
    def _():
        ext_ref[0:halo, :] = jnp.zeros((halo, d), F32)
        h_ref[...] = jnp.zeros((1, d), F32)

    x = x_ref[0]
    sh1 = mod_ref[0, 0:1, :]
    sc1 = mod_ref[0, 1:2, :]
    h = (x * _rms_scale(x) * g1_ref[...]) * (1.0 + sc1) + sh1
    hb = h.astype(BF16)

    half = d // 2

    def proj(col, part=None):
        c0 = col * d if part is None else col * d + part * half
        c1 = (col + 1) * d if part is None else c0 + half
        return jnp.dot(hb, w_ref[:, c0:c1], preferred_element_type=F32)

    def head_norm(p, gain, o_ref, first_head):
        for hh in range(p.shape[1] // HEAD_DIM):
            ph = p[:, hh * HEAD_DIM:(hh + 1) * HEAD_DIM]
            c0 = (first_head + hh) * HEAD_DIM
            o_ref[0, :, c0:c0 + HEAD_DIM] = (ph * _rms_scale(ph) * gain).astype(BF16)

    ext_ref[halo:halo + tm, :] = proj(0)
    q_proj = proj(2)
    rows = ext_ref[...]
    xc = cb_ref[...] + cw_ref[RNN_CONV - 1:RNN_CONV, :] * rows[halo:halo + tm]
    for kk in range(RNN_CONV - 1):
        shift = RNN_CONV - 1 - kk
        xc = xc + cw_ref[kk:kk + 1, :] * pltpu.roll(rows, shift, axis=0)[halo:halo + tm]
    ext_ref[0:halo, :] = ext_ref[tm:tm + halo, :]
    head_norm(q_proj, qg_ref[...] * (1.0 / math.sqrt(HEAD_DIM)), q_ref, 0)

    pieces = [(3, 0), (3, 1), (4, 0), (4, 1), (6, 0), (6, 1), (1, 0), (5, 0), (1, 1), (5, 1)]
    y_half = {}

    def finish(piece, val):
        col, part = piece
        cs = slice(part * half, (part + 1) * half)
        if col == 3:
            head_norm(val, kg_ref[...], k_ref, part * (ATT_HEADS // 2))
        elif col == 4:
            v_ref[0, :, cs] = val.astype(BF16)
        elif col == 6:
            sgb_ref[0, :, cs] = _sigmoid(val).astype(BF16)
        elif col == 1:
            y_half[part] = _gelu_tanh(val)
        else:
            ext_ref[halo:halo + tm, cs] = y_half[part] * _sigmoid(val)

    row = lax.broadcasted_iota(jnp.int32, (V7X_SUBLANES, RNN_BLOCK), 0)

    def scan_head(a_all, u_all, sl):
        carry = h_ref[:, sl]
        for g in range(tm // V7X_SUBLANES):
            rows = slice(g * V7X_SUBLANES, (g + 1) * V7X_SUBLANES)
            a, u = a_all[rows], u_all[rows]
            for dist in (1, 2, 4):
                keep = row >= dist
                a_prev = pltpu.roll(a, dist, axis=0)
                u_prev = pltpu.roll(u, dist, axis=0)
                u = jnp.where(keep, a * u_prev + u, u)
                a = jnp.where(keep, a * a_prev, a)
            hs = a * carry + u
            hs_ref[rows, sl] = hs
            carry = hs[V7X_SUBLANES - 1:V7X_SUBLANES, :]
        h_ref[:, sl] = carry

    decay = -RG_C * jnp.log(1.0 + jnp.exp(-lam_ref[...]))
    pending = None
    for hh in range(RNN_HEADS):
        sl = slice(hh * RNN_BLOCK, (hh + 1) * RNN_BLOCK)
        xh = xc[:, sl]
        ri = jnp.dot(xh.astype(BF16), wg_ref[hh], preferred_element_type=F32)
        current = (pieces[hh], proj(*pieces[hh]))
        r = _sigmoid(ri[:, :RNN_BLOCK] + ba_ref[:, sl])
        ig = _sigmoid(ri[:, RNN_BLOCK:] + bx_ref[:, sl])
        a = jnp.exp(r * decay[:, sl])
        scan_head(a, jnp.sqrt(1.0 - a * a) * (ig * xh), sl)
        if pending is not None:
            finish(*pending)
        pending = current
    for piece in pieces[RNN_HEADS:]:
        current = (piece, proj(*piece))
        finish(*pending)
        pending = current
    finish(*pending)

    yag_ref[0] = (hs_ref[...] * ext_ref[halo:halo + tm, :]).astype(BF16)


def _in_proj_rglru(x, mod, g1, w_in, conv_w, conv_b, w_gate, ba, bx, lam, qg, kg):
    b, s, d = x.shape
    tm = TM_IN
    tok_spec = pl.BlockSpec((1, tm, d), lambda bi, ti: (bi, ti, 0))
    act = jax.ShapeDtypeStruct((b, s, d), BF16)
    kern = functools.partial(_in_proj_kernel, tm=tm, d=d)
    return pl.pallas_call(
        kern,
        grid=(b, s // tm),
        in_specs=[
            tok_spec,
            pl.BlockSpec((1, 6, d), lambda bi, ti: (bi, 0, 0)),
            _const_spec((1, d)),
            _const_spec(w_in.shape),
            _const_spec(conv_w.shape),
            _const_spec((1, d)),
            _const_spec(w_gate.shape),
            _const_spec((1, d)),
            _const_spec((1, d)),
            _const_spec((1, d)),
            _const_spec((1, HEAD_DIM)),
            _const_spec((1, HEAD_DIM)),
        ],
        out_specs=[tok_spec] * 5,
        out_shape=[act] * 5,
        scratch_shapes=[
            pltpu.VMEM((tm + V7X_SUBLANES, d), F32),
            pltpu.VMEM((tm, d), F32),
            pltpu.VMEM((1, d), F32),
        ],
        compiler_params=pltpu.CompilerParams(
            dimension_semantics=("arbitrary", "arbitrary"),
            vmem_limit_bytes=_vmem_limit(60 * 1024 * 1024)),
        name="in_proj_rglru",
    )(x, mod, g1, w_in, conv_w, conv_b, w_gate, ba, bx, lam, qg, kg)


def _attn_kernel(q_ref, k_ref, v_ref, yag_ref, sgb_ref, sel_ref, o_ref,
                 acc_ref, r_ref, z_ref, p_ref, *, t, nq, heads):
    i = pl.program_id(2)
    row = lax.broadcasted_iota(jnp.int32, (t, t), 0)
    col = lax.broadcasted_iota(jnp.int32, (t, t), 1)
    below = row > col
    tri = below.astype(BF16)
    streams = [(s, g) for s in range(nq) for g in range(heads)]
    rows = [slice(s * t, (s + 1) * t) for s in range(nq)]
    cols = [slice(g * HEAD_DIM, (g + 1) * HEAD_DIM) for g in range(heads)]

    def key_start(j, s):
        return pl.multiple_of(jnp.maximum(j + s, 0) * t, t)

    def scores(j, group=None):
        for u in (range(len(streams)) if group is None else group):
            s, g = streams[u]
            z_ref[u] = lax.dot_general(
                q_ref[0, rows[s], cols[g]], k_ref[0, pl.ds(key_start(j, s), t), cols[g]],
                (((1,), (1,)), ((), ())), preferred_element_type=F32)

    def weighted_values(j):
        for u, (s, g) in enumerate(streams):
            acc_ref[rows[s], cols[g]] += jnp.dot(
                p_ref[u], v_ref[0, pl.ds(key_start(j, s), t), cols[g]],
                preferred_element_type=F32)

    def weights(j, on_diagonal):
        log_betas, sums, totals = {}, {}, None
        n_groups = 4
        per_group = len(streams) // n_groups
        for gi in range(n_groups):
            group = range(gi * per_group, (gi + 1) * per_group)
            parts = []
            for u in group:
                z = z_ref[u]
                log_beta = jnp.minimum(z, 0.0) - jnp.log(1.0 + jnp.exp(-jnp.abs(z)))
                log_1mb = log_beta - z
                if on_diagonal:
                    log_1mb = jnp.where(below, log_1mb, 0.0)
                log_betas[u] = log_beta
                parts.append(log_1mb.astype(BF16))
            group_sums = jnp.dot(jnp.concatenate(parts, axis=0), tri,
                                 preferred_element_type=F32)
            for n, u in enumerate(group):
                sums[u] = group_sums[n * t:(n + 1) * t]
            group_totals = jnp.dot(
                jnp.concatenate(parts, axis=1),
                sel_ref[gi * per_group * t:(gi + 1) * per_group * t, :],
                preferred_element_type=F32)
            totals = group_totals if totals is None else totals + group_totals
            scores(j - 1, group)
        r_old = jnp.where(lane >= -j * heads, r_ref[...], OUT_OF_KEYS)
        for u in range(len(streams)):
            attn = jnp.exp(log_betas[u] + sums[u] + r_old[:, u:u + 1])
            if on_diagonal:
                attn = jnp.where(below, attn, 0.0)
            p_ref[u] = attn.astype(BF16)
        r_new = r_old + totals
        r_ref[...] = r_new
        live = jnp.logical_and(lane < len(streams), r_new >= F32_EXP_ZERO_BELOW)
        return jnp.max(live.astype(jnp.int32))

    lane = lax.broadcasted_iota(jnp.int32, (t, V7X_LANES), 1)
    acc_ref[...] = jnp.zeros_like(acc_ref)
    r_ref[...] = jnp.zeros_like(r_ref)
    j0 = i * nq
    scores(j0)
    live0 = weights(j0, True)

    def cond(c):
        j, live = c
        return jnp.logical_and(j + (nq - 1) >= 0, live > 0)

    def body(c):
        j, _ = c
        weighted_values(j + 1)
        return j - 1, weights(j, False)

    j_end, _ = lax.while_loop(cond, body, (j0 - 1, live0))
    weighted_values(j_end + 1)
    mix = yag_ref[0].astype(F32) + sgb_ref[0].astype(F32) * acc_ref[...]
    o_ref[0] = mix.astype(BF16)


def _stickbreak(q, k, v, yag, sgb):
    b, s, d = q.shape
    t = T_ATT
    nq = ATT_Q_BLOCKS_PER_STEP
    hg = ATT_HEADS_PER_STEP
    kern = functools.partial(_attn_kernel, t=t, nq=nq, heads=hg)
    n_streams = nq * hg
    assert n_streams <= V7X_LANES
    sel = (jnp.arange(n_streams * t)[:, None] // t == jnp.arange(V7X_LANES)[None, :]).astype(BF16)
    q_spec = pl.BlockSpec((1, nq * t, hg * HEAD_DIM), lambda bi, hi, qi: (bi, qi, hi))
    kv_spec = pl.BlockSpec((1, s, hg * HEAD_DIM), lambda bi, hi, qi: (bi, 0, hi))
    return pl.pallas_call(
        kern,
        grid=(b, ATT_HEADS // hg, s // (nq * t)),
        in_specs=[q_spec, kv_spec, kv_spec, q_spec, q_spec, _const_spec(sel.shape)],
        out_specs=q_spec,
        out_shape=jax.ShapeDtypeStruct((b, s, d), BF16),
        scratch_shapes=[
            pltpu.VMEM((nq * t, hg * HEAD_DIM), F32),
            pltpu.VMEM((t, V7X_LANES), F32),
            pltpu.VMEM((nq * hg, t, t), F32),
            pltpu.VMEM((nq * hg, t, t), BF16),
        ],
        compiler_params=pltpu.CompilerParams(
            dimension_semantics=("arbitrary", "arbitrary", "arbitrary"),
            vmem_limit_bytes=_vmem_limit(60 * 1024 * 1024)),
        name="stickbreak",
    )(q, k, v, yag, sgb, sel)


def _mix_ffn_kernel(x_ref, mix_ref, mod_ref, wo_ref, g2_ref, wu_ref,
                    cw_ref, cb_ref, wd_ref, o_ref, ext_ref, hist_ref, act_ref, *, tm, d, dff):
    t_idx = pl.program_id(1)
    halo = V7X_SUBLANES

    @pl.when(t_idx == 0)
    def _():
        hist_ref[...] = jnp.zeros_like(hist_ref)

    gt1 = mod_ref[0, 2:3, :]
    sh2 = mod_ref[0, 3:4, :]
    sc2 = mod_ref[0, 4:5, :]
    gt2 = mod_ref[0, 5:6, :]

    x1 = x_ref[0] + gt1 * jnp.dot(mix_ref[0], wo_ref[...], preferred_element_type=F32)
    h2 = ((x1 * _rms_scale(x1) * g2_ref[...]) * (1.0 + sc2) + sh2).astype(BF16)

    widths = [FF_CHUNK] * (dff // FF_CHUNK)
    starts = [sum(widths[:c]) for c in range(len(widths))]

    def col_slices(c):
        return (slice(starts[c], starts[c] + widths[c]),
                slice(dff + starts[c], dff + starts[c] + widths[c]))

    def up_proj(c):
        for part, cs in enumerate(col_slices(c)):
            buf = 2 * (c % 2) + part
            ext_ref[buf, 0:halo, 0:widths[c]] = hist_ref[:, cs]
            ext_ref[buf, halo:halo + tm, 0:widths[c]] = jnp.dot(h2, wu_ref[:, cs],
                                                                preferred_element_type=F32)
            hist_ref[:, cs] = ext_ref[buf, tm:tm + halo, 0:widths[c]]

    def conv(c, part):
        cs = col_slices(c)[part]
        buf = 2 * (c % 2) + part
        rows = ext_ref[buf, :, 0:widths[c]]
        out = cb_ref[:, cs] + cw_ref[FFN_CONV - 1:FFN_CONV, cs] * rows[halo:halo + tm]
        for kk in range(FFN_CONV - 1):
            shift = FFN_CONV - 1 - kk
            earlier = pltpu.roll(rows, shift, axis=0)[halo:halo + tm]
            out = out + cw_ref[kk:kk + 1, cs] * earlier
        return out

    def down_proj(c):
        return jnp.dot(act_ref[c % 2, :, 0:widths[c]],
                       wd_ref[starts[c]:starts[c] + widths[c], :],
                       preferred_element_type=F32)

    n_chunks = len(widths)
    acc = jnp.zeros((tm, d), F32)
    up_proj(0)
    for c in range(n_chunks):
        if c + 1 < n_chunks:
            up_proj(c + 1)
        if c > 0:
            acc = acc + down_proj(c - 1)
        act_ref[c % 2, :, 0:widths[c]] = (_gelu_tanh(conv(c, 0)) * conv(c, 1)).astype(BF16)
    acc = acc + down_proj(n_chunks - 1)
    o_ref[0] = x1 + gt2 * acc


def _mix_ffn(x, mix, mod, w_out, g2, w_up, conv_w, conv_b, w_down):
    b, s, d = x.shape
    dff = w_down.shape[0]
    tm = TM_FFN
    tok_spec = pl.BlockSpec((1, tm, d), lambda bi, ti: (bi, ti, 0))
    kern = functools.partial(_mix_ffn_kernel, tm=tm, d=d, dff=dff)
    return pl.pallas_call(
        kern,
        grid=(b, s // tm),
        in_specs=[
            tok_spec, tok_spec,
            pl.BlockSpec((1, 6, d), lambda bi, ti: (bi, 0, 0)),
            _const_spec(w_out.shape),
            _const_spec((1, d)),
            _const_spec(w_up.shape),
            _const_spec(conv_w.shape),
            _const_spec((1, 2 * dff)),
            _const_spec(w_down.shape),
        ],
        out_specs=tok_spec,
        out_shape=jax.ShapeDtypeStruct((b, s, d), F32),
        scratch_shapes=[
            pltpu.VMEM((4, tm + V7X_SUBLANES, FF_CHUNK), F32),
            pltpu.VMEM((V7X_SUBLANES, 2 * dff), F32),
            pltpu.VMEM((2, tm, FF_CHUNK), BF16),
        ],
        compiler_params=pltpu.CompilerParams(
            dimension_semantics=("arbitrary", "arbitrary"),
            vmem_limit_bytes=_vmem_limit(60 * 1024 * 1024)),
        name="mix_ffn",
    )(x, mix, mod, w_out, g2, w_up, conv_w, conv_b, w_down)


def kernel(x, c, ada_w, ada_b, norm1_g, w_in, conv_w, conv_b, rg_wa, rg_ba, rg_wx, rg_bx,
           rg_lambda, q_norm_g, k_norm_g, w_out, norm2_g, ffn_up, ffn_conv_w, ffn_conv_b,
           ffn_down):
    depth, d, _ = ada_w.shape
    b = x.shape[0]
    for tile in (TM_IN, TM_FFN, T_ATT * ATT_Q_BLOCKS_PER_STEP):
        assert x.shape[1] % tile == 0
    assert d == ATT_HEADS * HEAD_DIM

    mod_all = _adaln_mod(c, ada_w, ada_b).reshape(depth, b, 6, d)
    for l in range(depth):
        mod = mod_all[l]
        w_gate = jnp.concatenate([rg_wa[l], rg_wx[l]], axis=-1).astype(BF16)
        q, k, v, yag, sgb = _in_proj_rglru(
            x, mod, norm1_g[l][None], w_in[l].astype(BF16), conv_w[l], conv_b[l][None],
            w_gate, rg_ba[l][None], rg_bx[l][None], rg_lambda[l][None],
            q_norm_g[l][None], k_norm_g[l][None])
        mix = _stickbreak(q, k, v, yag, sgb)
        x = _mix_ffn(x, mix, mod, w_out[l].astype(BF16), norm2_g[l][None],
                     ffn_up[l].astype(BF16), ffn_conv_w[l], ffn_conv_b[l][None],
                     ffn_down[l].astype(BF16))
    return x
```

```python
import functools
import math

import jax
import jax.numpy as jnp
from jax import lax
from jax.experimental import pallas as pl
from jax.experimental.pallas import tpu as pltpu

F32 = jnp.float32
BF16 = jnp.bfloat16

V7X_LANES = 128
V7X_SUBLANES = 8
V7X_VMEM_BYTES = 64 * 1024 * 1024

RNN_HEADS = 8
RNN_BLOCK = 128
RNN_CONV = 4
RG_C = 8.0
ATT_HEADS = 8
HEAD_DIM = 128
FFN_CONV = 3
EPS = 1e-6

F32_EXP_ZERO_BELOW = -105.0
OUT_OF_KEYS = -1e30

TM_IN = 512
TM_FFN = 512
T_ATT = 128
ATT_Q_BLOCKS_PER_STEP = 8
ATT_HEADS_PER_STEP = 4
FF_CHUNK = 512
MOD_TN = 1536


def _vmem_limit(nbytes):
    return int(min(nbytes, V7X_VMEM_BYTES - 4 * 1024 * 1024))


def _const_spec(shape):
    zeros = (0,) * len(shape)
    return pl.BlockSpec(shape, lambda *_: zeros, pipeline_mode=pl.Buffered(1))


def _layer_spec(stacked, layer):
    shape = stacked.shape[1:]
    index = (layer,) + (0,) * len(shape)
    return pl.BlockSpec((None,) + shape, lambda *_: index, pipeline_mode=pl.Buffered(1))


def _mod_spec(mod_all, layer):
    _, _, six, d = mod_all.shape
    return pl.BlockSpec((None, 1, six, d), lambda bi, ti: (layer, bi, 0, 0))


def _gelu_tanh(x):
    c = math.sqrt(2.0 / math.pi)
    return 0.5 * x * (1.0 + jnp.tanh(c * (x + 0.044715 * (x * x * x))))


def _sigmoid(x):
    return 1.0 / (1.0 + jnp.exp(-x))


def _rms_scale(x):
    return lax.rsqrt(jnp.mean(x * x, axis=-1, keepdims=True) + EPS)


def _mod_kernel(c_ref, w_ref, b_ref, o_ref):
    acc = jnp.dot(c_ref[...].astype(BF16), w_ref[0].astype(BF16),
                  preferred_element_type=F32)
    o_ref[0] = acc + b_ref[0]


def _adaln_mod(c, ada_w, ada_b):
    depth, d, n = ada_w.shape
    b = c.shape[0]
    return pl.pallas_call(
        _mod_kernel,
        grid=(depth, n // MOD_TN),
        in_specs=[
            pl.BlockSpec((b, d), lambda l, j: (0, 0)),
            pl.BlockSpec((1, d, MOD_TN), lambda l, j: (l, 0, j)),
            pl.BlockSpec((1, 1, MOD_TN), lambda l, j: (l, 0, j)),
        ],
        out_specs=pl.BlockSpec((1, b, MOD_TN), lambda l, j: (l, 0, j)),
        out_shape=jax.ShapeDtypeStruct((depth, b, n), F32),
        compiler_params=pltpu.CompilerParams(
            dimension_semantics=("arbitrary", "arbitrary"),
            vmem_limit_bytes=_vmem_limit(32 * 1024 * 1024)),
        name="adaln_mod",
    )(c, ada_w, ada_b.reshape(depth, 1, n))


def _in_proj_kernel(x_ref, mod_ref, g1_ref, w_ref, cw_ref, cb_ref, wg_ref, ba_ref, bx_ref,
                    lam_ref, qg_ref, kg_ref,
                    q_ref, k_ref, v_ref, yag_ref, sgb_ref,
                    ext_ref, hs_ref, h_ref, *, tm, d):
    t_idx = pl.program_id(1)
    halo = V7X_SUBLANES

    @pl.when(t_idx == 0)
    def _():
        ext_ref[0:halo, :] = jnp.zeros((halo, d), F32)
        h_ref[...] = jnp.zeros((1, d), F32)

    x = x_ref[0]
    sh1 = mod_ref[0, 0:1, :]
    sc1 = mod_ref[0, 1:2, :]
    h = (x * _rms_scale(x) * g1_ref[...]) * (1.0 + sc1) + sh1
    hb = h.astype(BF16)

    half = d // 2

    def proj(col, part=None):
        c0 = col * d if part is None else col * d + part * half
        c1 = (col + 1) * d if part is None else c0 + half
        return jnp.dot(hb, w_ref[:, c0:c1], preferred_element_type=F32)

    def head_norm(p, gain, o_ref, first_head):
        for hh in range(p.shape[1] // HEAD_DIM):
            ph = p[:, hh * HEAD_DIM:(hh + 1) * HEAD_DIM]
            c0 = (first_head + hh) * HEAD_DIM
            o_ref[0, :, c0:c0 + HEAD_DIM] = (ph * _rms_scale(ph) * gain).astype(BF16)

    ext_ref[halo:halo + tm, :] = proj(0)
    q_proj = proj(2)
    rows = ext_ref[...]
    xc = cb_ref[...] + cw_ref[RNN_CONV - 1:RNN_CONV, :] * rows[halo:halo + tm]
    for kk in range(RNN_CONV - 1):
        shift = RNN_CONV - 1 - kk
        xc = xc + cw_ref[kk:kk + 1, :] * pltpu.roll(rows, shift, axis=0)[halo:halo + tm]
    ext_ref[0:halo, :] = ext_ref[tm:tm + halo, :]
    head_norm(q_proj, qg_ref[...] * (1.0 / math.sqrt(HEAD_DIM)), q_ref, 0)

    pieces = [(3, 0), (3, 1), (4, 0), (4, 1), (6, 0), (6, 1), (1, 0), (5, 0), (1, 1), (5, 1)]
    y_half = {}

    def finish(piece, val):
        col, part = piece
        cs = slice(part * half, (part + 1) * half)
        if col == 3:
            head_norm(val, kg_ref[...], k_ref, part * (ATT_HEADS // 2))
        elif col == 4:
            v_ref[0, :, cs] = val.astype(BF16)
        elif col == 6:
            sgb_ref[0, :, cs] = _sigmoid(val).astype(BF16)
        elif col == 1:
            y_half[part] = _gelu_tanh(val)
        else:
            ext_ref[halo:halo + tm, cs] = y_half[part] * _sigmoid(val)

    row = lax.broadcasted_iota(jnp.int32, (V7X_SUBLANES, RNN_BLOCK), 0)

    def scan_head(a_all, u_all, sl):
        carry = h_ref[:, sl]
        for g in range(tm // V7X_SUBLANES):
            rows = slice(g * V7X_SUBLANES, (g + 1) * V7X_SUBLANES)
            a, u = a_all[rows], u_all[rows]
            for dist in (1, 2, 4):
                keep = row >= dist
                a_prev = pltpu.roll(a, dist, axis=0)
                u_prev = pltpu.roll(u, dist, axis=0)
                u = jnp.where(keep, a * u_prev + u, u)
                a = jnp.where(keep, a * a_prev, a)
            hs = a * carry + u
            hs_ref[rows, sl] = hs
            carry = hs[V7X_SUBLANES - 1:V7X_SUBLANES, :]
        h_ref[:, sl] = carry

    decay = -RG_C * jnp.log(1.0 + jnp.exp(-lam_ref[...]))
    pending = None
    for hh in range(RNN_HEADS):
        sl = slice(hh * RNN_BLOCK, (hh + 1) * RNN_BLOCK)
        xh = xc[:, sl]
        ri = jnp.dot(xh.astype(BF16), wg_ref[hh], preferred_element_type=F32)
        current = (pieces[hh], proj(*pieces[hh]))
        r = _sigmoid(ri[:, :RNN_BLOCK] + ba_ref[:, sl])
        ig = _sigmoid(ri[:, RNN_BLOCK:] + bx_ref[:, sl])
        a = jnp.exp(r * decay[:, sl])
        scan_head(a, jnp.sqrt(1.0 - a * a) * (ig * xh), sl)
        if pending is not None:
            finish(*pending)
        pending = current
    for piece in pieces[RNN_HEADS:]:
        current = (piece, proj(*piece))
        finish(*pending)
        pending = current
    finish(*pending)

    yag_ref[0] = (hs_ref[...] * ext_ref[halo:halo + tm, :]).astype(BF16)


def _in_proj_rglru(layer, x, mod_all, *params):
    b, s, d = x.shape
    tm = TM_IN
    tok_spec = pl.BlockSpec((1, tm, d), lambda bi, ti: (bi, ti, 0))
    act = jax.ShapeDtypeStruct((b, s, d), BF16)
    kern = functools.partial(_in_proj_kernel, tm=tm, d=d)
    return pl.pallas_call(
        kern,
        grid=(b, s // tm),
        in_specs=[
            tok_spec,
            _mod_spec(mod_all, layer),
        ] + [_layer_spec(p, layer) for p in params],
        out_specs=[tok_spec] * 5,
        out_shape=[act] * 5,
        scratch_shapes=[
            pltpu.VMEM((tm + V7X_SUBLANES, d), F32),
            pltpu.VMEM((tm, d), F32),
            pltpu.VMEM((1, d), F32),
        ],
        compiler_params=pltpu.CompilerParams(
            dimension_semantics=("arbitrary", "arbitrary"),
            vmem_limit_bytes=_vmem_limit(60 * 1024 * 1024)),
        name="in_proj_rglru",
    )(x, mod_all, *params)


def _attn_kernel(q_ref, k_ref, v_ref, yag_ref, sgb_ref, sel_ref, o_ref,
                 acc_ref, r_ref, z_ref, p_ref, *, t, nq, heads):
    i = pl.program_id(2)
    row = lax.broadcasted_iota(jnp.int32, (t, t), 0)
    col = lax.broadcasted_iota(jnp.int32, (t, t), 1)
    below = row > col
    tri = below.astype(BF16)
    streams = [(s, g) for s in range(nq) for g in range(heads)]
    rows = [slice(s * t, (s + 1) * t) for s in range(nq)]
    cols = [slice(g * HEAD_DIM, (g + 1) * HEAD_DIM) for g in range(heads)]

    def key_start(j, s):
        return pl.multiple_of(jnp.maximum(j + s, 0) * t, t)

    def scores(j, group=None):
        for u in (range(len(streams)) if group is None else group):
            s, g = streams[u]
            z_ref[u] = lax.dot_general(
                q_ref[0, rows[s], cols[g]], k_ref[0, pl.ds(key_start(j, s), t), cols[g]],
                (((1,), (1,)), ((), ())), preferred_element_type=F32)

    def weighted_values(j):
        for u, (s, g) in enumerate(streams):
            acc_ref[rows[s], cols[g]] += jnp.dot(
                p_ref[u], v_ref[0, pl.ds(key_start(j, s), t), cols[g]],
                preferred_element_type=F32)

    def weights(j, on_diagonal):
        log_betas, sums, totals = {}, {}, None
        n_groups = 4
        per_group = len(streams) // n_groups
        for gi in range(n_groups):
            group = range(gi * per_group, (gi + 1) * per_group)
            parts = []
            for u in group:
                z = z_ref[u]
                log_beta = jnp.minimum(z, 0.0) - jnp.log(1.0 + jnp.exp(-jnp.abs(z)))
                log_1mb = log_beta - z
                if on_diagonal:
                    log_1mb = jnp.where(below, log_1mb, 0.0)
                log_betas[u] = log_beta
                parts.append(log_1mb.astype(BF16))
            group_sums = jnp.dot(jnp.concatenate(parts, axis=0), tri,
                                 preferred_element_type=F32)
            for n, u in enumerate(group):
                sums[u] = group_sums[n * t:(n + 1) * t]
            group_totals = jnp.dot(
                jnp.concatenate(parts, axis=1),
                sel_ref[gi * per_group * t:(gi + 1) * per_group * t, :],
                preferred_element_type=F32)
            totals = group_totals if totals is None else totals + group_totals
            scores(j - 1, group)
        r_old = jnp.where(lane >= -j * heads, r_ref[...], OUT_OF_KEYS)
        for u in range(len(streams)):
            attn = jnp.exp(log_betas[u] + sums[u] + r_old[:, u:u + 1])
            if on_diagonal:
                attn = jnp.where(below, attn, 0.0)
            p_ref[u] = attn.astype(BF16)
        r_new = r_old + totals
        r_ref[...] = r_new
        live = jnp.logical_and(lane < len(streams), r_new >= F32_EXP_ZERO_BELOW)
        return jnp.max(live.astype(jnp.int32))

    lane = lax.broadcasted_iota(jnp.int32, (t, V7X_LANES), 1)
    acc_ref[...] = jnp.zeros_like(acc_ref)
    r_ref[...] = jnp.zeros_like(r_ref)
    j0 = i * nq
    scores(j0)
    live0 = weights(j0, True)

    def cond(c):
        j, live = c
        return jnp.logical_and(j + (nq - 1) >= 0, live > 0)

    def body(c):
        j, _ = c
        weighted_values(j + 1)
        return j - 1, weights(j, False)

    j_end, _ = lax.while_loop(cond, body, (j0 - 1, live0))
    weighted_values(j_end + 1)
    mix = yag_ref[0].astype(F32) + sgb_ref[0].astype(F32) * acc_ref[...]
    o_ref[0] = mix.astype(BF16)


def _stickbreak(q, k, v, yag, sgb):
    b, s, d = q.shape
    t = T_ATT
    nq = ATT_Q_BLOCKS_PER_STEP
    hg = ATT_HEADS_PER_STEP
    kern = functools.partial(_attn_kernel, t=t, nq=nq, heads=hg)
    n_streams = nq * hg
    assert n_streams <= V7X_LANES
    sel = (jnp.arange(n_streams * t)[:, None] // t == jnp.arange(V7X_LANES)[None, :]).astype(BF16)
    q_spec = pl.BlockSpec((1, nq * t, hg * HEAD_DIM), lambda bi, hi, qi: (bi, qi, hi))
    kv_spec = pl.BlockSpec((1, s, hg * HEAD_DIM), lambda bi, hi, qi: (bi, 0, hi))
    return pl.pallas_call(
        kern,
        grid=(b, ATT_HEADS // hg, s // (nq * t)),
        in_specs=[q_spec, kv_spec, kv_spec, q_spec, q_spec, _const_spec(sel.shape)],
        out_specs=q_spec,
        out_shape=jax.ShapeDtypeStruct((b, s, d), BF16),
        scratch_shapes=[
            pltpu.VMEM((nq * t, hg * HEAD_DIM), F32),
            pltpu.VMEM((t, V7X_LANES), F32),
            pltpu.VMEM((nq * hg, t, t), F32),
            pltpu.VMEM((nq * hg, t, t), BF16),
        ],
        compiler_params=pltpu.CompilerParams(
            dimension_semantics=("arbitrary", "arbitrary", "arbitrary"),
            vmem_limit_bytes=_vmem_limit(60 * 1024 * 1024)),
        name="stickbreak",
    )(q, k, v, yag, sgb, sel)


def _mix_ffn_kernel(x_ref, mix_ref, mod_ref, wo_ref, g2_ref, wu_ref,
                    cw_ref, cb_ref, wd_ref, o_ref, ext_ref, hist_ref, act_ref, *, tm, d, dff):
    t_idx = pl.program_id(1)
    halo = V7X_SUBLANES

    @pl.when(t_idx == 0)
    def _():
        hist_ref[...] = jnp.zeros_like(hist_ref)

    gt1 = mod_ref[0, 2:3, :]
    sh2 = mod_ref[0, 3:4, :]
    sc2 = mod_ref[0, 4:5, :]
    gt2 = mod_ref[0, 5:6, :]

    x1 = x_ref[0] + gt1 * jnp.dot(mix_ref[0], wo_ref[...], preferred_element_type=F32)
    h2 = ((x1 * _rms_scale(x1) * g2_ref[...]) * (1.0 + sc2) + sh2).astype(BF16)

    widths = [FF_CHUNK] * (dff // FF_CHUNK)
    starts = [sum(widths[:c]) for c in range(len(widths))]

    def col_slices(c):
        return (slice(starts[c], starts[c] + widths[c]),
                slice(dff + starts[c], dff + starts[c] + widths[c]))

    def up_proj(c):
        for part, cs in enumerate(col_slices(c)):
            buf = 2 * (c % 2) + part
            ext_ref[buf, 0:halo, 0:widths[c]] = hist_ref[:, cs]
            ext_ref[buf, halo:halo + tm, 0:widths[c]] = jnp.dot(h2, wu_ref[:, cs],
                                                                preferred_element_type=F32)
            hist_ref[:, cs] = ext_ref[buf, tm:tm + halo, 0:widths[c]]

    def conv(c, part):
        cs = col_slices(c)[part]
        buf = 2 * (c % 2) + part
        out = cb_ref[:, cs] + cw_ref[FFN_CONV - 1:FFN_CONV, cs] * ext_ref[buf, halo:halo + tm, :]
        for kk in range(FFN_CONV - 1):
            shift = FFN_CONV - 1 - kk
            out = out + cw_ref[kk:kk + 1, cs] * ext_ref[buf, halo - shift:halo - shift + tm, :]
        return out

    def down_proj(c):
        return jnp.dot(act_ref[c % 2, :, 0:widths[c]],
                       wd_ref[starts[c]:starts[c] + widths[c], :],
                       preferred_element_type=F32)

    n_chunks = len(widths)
    acc = jnp.zeros((tm, d), F32)
    up_proj(0)
    for c in range(n_chunks):
        if c + 1 < n_chunks:
            up_proj(c + 1)
        if c > 0:
            acc = acc + down_proj(c - 1)
        act_ref[c % 2, :, 0:widths[c]] = (_gelu_tanh(conv(c, 0)) * conv(c, 1)).astype(BF16)
    acc = acc + down_proj(n_chunks - 1)
    o_ref[0] = x1 + gt2 * acc


def _mix_ffn(layer, x, mix, mod_all, *params):
    b, s, d = x.shape
    dff = params[-1].shape[1]
    tm = TM_FFN
    tok_spec = pl.BlockSpec((1, tm, d), lambda bi, ti: (bi, ti, 0))
    kern = functools.partial(_mix_ffn_kernel, tm=tm, d=d, dff=dff)
    return pl.pallas_call(
        kern,
        grid=(b, s // tm),
        in_specs=[
            tok_spec, tok_spec,
            _mod_spec(mod_all, layer),
        ] + [_layer_spec(p, layer) for p in params],
        out_specs=tok_spec,
        out_shape=jax.ShapeDtypeStruct((b, s, d), F32),
        scratch_shapes=[
            pltpu.VMEM((4, tm + V7X_SUBLANES, FF_CHUNK), F32),
            pltpu.VMEM((V7X_SUBLANES, 2 * dff), F32),
            pltpu.VMEM((2, tm, FF_CHUNK), BF16),
        ],
        compiler_params=pltpu.CompilerParams(
            dimension_semantics=("arbitrary", "arbitrary"),
            vmem_limit_bytes=_vmem_limit(60 * 1024 * 1024)),
        name="mix_ffn",
    )(x, mix, mod_all, *params)


def kernel(x, c, ada_w, ada_b, norm1_g, w_in, conv_w, conv_b, rg_wa, rg_ba, rg_wx, rg_bx,
           rg_lambda, q_norm_g, k_norm_g, w_out, norm2_g, ffn_up, ffn_conv_w, ffn_conv_b,
           ffn_down):
    depth, d, _ = ada_w.shape
    b = x.shape[0]
    for tile in (TM_IN, TM_FFN, T_ATT * ATT_Q_BLOCKS_PER_STEP):
        assert x.shape[1] % tile == 0
    assert d == ATT_HEADS * HEAD_DIM

    mod_all = _adaln_mod(c, ada_w, ada_b).reshape(depth, b, 6, d)

    def rows(p):
        return p.reshape(depth, 1, p.shape[-1])

    in_params = (rows(norm1_g), w_in.astype(BF16), conv_w, rows(conv_b),
                 jnp.concatenate([rg_wa, rg_wx], axis=-1).astype(BF16),
                 rows(rg_ba), rows(rg_bx), rows(rg_lambda), rows(q_norm_g), rows(k_norm_g))
    ffn_params = (w_out.astype(BF16), rows(norm2_g), ffn_up.astype(BF16), ffn_conv_w,
                  rows(ffn_conv_b), ffn_down.astype(BF16))
    for l in range(depth):
        q, k, v, yag, sgb = _in_proj_rglru(l, x, mod_all, *in_params)
        mix = _stickbreak(q, k, v, yag, sgb)
        x = _mix_ffn(l, x, mix, mod_all, *ffn_params)
    return x
```

```python
import functools
import math

import jax
import jax.numpy as jnp
from jax import lax
from jax.experimental import pallas as pl
from jax.experimental.pallas import tpu as pltpu

F32 = jnp.float32
BF16 = jnp.bfloat16

V7X_LANES = 128
V7X_SUBLANES = 8
V7X_VMEM_BYTES = 64 * 1024 * 1024

RNN_HEADS = 8
RNN_BLOCK = 128
RNN_CONV = 4
RG_C = 8.0
ATT_HEADS = 8
HEAD_DIM = 128
FFN_CONV = 3
EPS = 1e-6

F32_EXP_ZERO_BELOW = -105.0
OUT_OF_KEYS = -1e30

TM_IN = 512
TM_FFN = 512
T_ATT = 128
ATT_Q_BLOCKS_PER_STEP = 16
ATT_HEADS_PER_STEP = 2
FF_CHUNK = 512
MOD_TN = 1536


def _vmem_limit(nbytes):
    return int(min(nbytes, V7X_VMEM_BYTES - 4 * 1024 * 1024))


def _const_spec(shape):
    zeros = (0,) * len(shape)
    return pl.BlockSpec(shape, lambda *_: zeros, pipeline_mode=pl.Buffered(1))


def _layer_spec(stacked, layer):
    shape = stacked.shape[1:]
    index = (layer,) + (0,) * len(shape)
    return pl.BlockSpec((None,) + shape, lambda *_: index, pipeline_mode=pl.Buffered(1))


def _mod_spec(mod_all, layer):
    _, _, six, d = mod_all.shape
    return pl.BlockSpec((None, 1, six, d), lambda bi, ti: (layer, bi, 0, 0))


def _gelu_tanh(x):
    c = math.sqrt(2.0 / math.pi)
    return 0.5 * x * (1.0 + jnp.tanh(c * (x + 0.044715 * (x * x * x))))


def _sigmoid(x):
    return 1.0 / (1.0 + jnp.exp(-x))


def _rms_scale(x):
    return lax.rsqrt(jnp.mean(x * x, axis=-1, keepdims=True) + EPS)


def _mod_kernel(c_ref, w_ref, b_ref, o_ref):
    acc = jnp.dot(c_ref[...].astype(BF16), w_ref[0].astype(BF16),
                  preferred_element_type=F32)
    o_ref[0] = acc + b_ref[0]


def _adaln_mod(c, ada_w, ada_b):
    depth, d, n = ada_w.shape
    b = c.shape[0]
    return pl.pallas_call(
        _mod_kernel,
        grid=(depth, n // MOD_TN),
        in_specs=[
            pl.BlockSpec((b, d), lambda l, j: (0, 0)),
            pl.BlockSpec((1, d, MOD_TN), lambda l, j: (l, 0, j)),
            pl.BlockSpec((1, 1, MOD_TN), lambda l, j: (l, 0, j)),
        ],
        out_specs=pl.BlockSpec((1, b, MOD_TN), lambda l, j: (l, 0, j)),
        out_shape=jax.ShapeDtypeStruct((depth, b, n), F32),
        compiler_params=pltpu.CompilerParams(
            dimension_semantics=("arbitrary", "arbitrary"),
            vmem_limit_bytes=_vmem_limit(32 * 1024 * 1024)),
        name="adaln_mod",
    )(c, ada_w, ada_b.reshape(depth, 1, n))


def _in_proj_kernel(x_ref, mod_ref, g1_ref, w_ref, cw_ref, cb_ref, wg_ref, ba_ref, bx_ref,
                    lam_ref, qg_ref, kg_ref,
                    q_ref, k_ref, v_ref, yag_ref, sgb_ref,
                    ext_ref, hs_ref, h_ref, *, tm, d):
    t_idx = pl.program_id(1)
    halo = V7X_SUBLANES

    @pl.when(t_idx == 0)
    def _():
        ext_ref[0:halo, :] = jnp.zeros((halo, d), F32)
        h_ref[...] = jnp.zeros((1, d), F32)

    x = x_ref[0]
    sh1 = mod_ref[0, 0:1, :]
    sc1 = mod_ref[0, 1:2, :]
    h = (x * _rms_scale(x) * g1_ref[...]) * (1.0 + sc1) + sh1
    hb = h.astype(BF16)

    half = d // 2

    def proj(col, part=None):
        c0 = col * d if part is None else col * d + part * half
        c1 = (col + 1) * d if part is None else c0 + half
        return jnp.dot(hb, w_ref[:, c0:c1], preferred_element_type=F32)

    def head_norm(p, gain, o_ref, first_head):
        for hh in range(p.shape[1] // HEAD_DIM):
            ph = p[:, hh * HEAD_DIM:(hh + 1) * HEAD_DIM]
            c0 = (first_head + hh) * HEAD_DIM
            o_ref[0, :, c0:c0 + HEAD_DIM] = (ph * _rms_scale(ph) * gain).astype(BF16)

    ext_ref[halo:halo + tm, :] = proj(0)
    q_proj = proj(2)
    rows = ext_ref[...]
    xc = cb_ref[...] + cw_ref[RNN_CONV - 1:RNN_CONV, :] * rows[halo:halo + tm]
    for kk in range(RNN_CONV - 1):
        shift = RNN_CONV - 1 - kk
        xc = xc + cw_ref[kk:kk + 1, :] * pltpu.roll(rows, shift, axis=0)[halo:halo + tm]
    ext_ref[0:halo, :] = ext_ref[tm:tm + halo, :]
    head_norm(q_proj, qg_ref[...] * (1.0 / math.sqrt(HEAD_DIM)), q_ref, 0)

    pieces = [(3, 0), (3, 1), (4, 0), (4, 1), (6, 0), (6, 1), (1, 0), (5, 0), (1, 1), (5, 1)]
    y_half = {}

    def finish(piece, val):
        col, part = piece
        cs = slice(part * half, (part + 1) * half)
        if col == 3:
            head_norm(val, kg_ref[...], k_ref, part * (ATT_HEADS // 2))
        elif col == 4:
            v_ref[0, :, cs] = val.astype(BF16)
        elif col == 6:
            sgb_ref[0, :, cs] = _sigmoid(val).astype(BF16)
        elif col == 1:
            y_half[part] = _gelu_tanh(val)
        else:
            ext_ref[halo:halo + tm, cs] = y_half[part] * _sigmoid(val)

    row = lax.broadcasted_iota(jnp.int32, (V7X_SUBLANES, RNN_BLOCK), 0)

    def scan_head(a_all, u_all, sl):
        carry = h_ref[:, sl]
        for g in range(tm // V7X_SUBLANES):
            rows = slice(g * V7X_SUBLANES, (g + 1) * V7X_SUBLANES)
            a, u = a_all[rows], u_all[rows]
            for dist in (1, 2, 4):
                keep = row >= dist
                a_prev = pltpu.roll(a, dist, axis=0)
                u_prev = pltpu.roll(u, dist, axis=0)
                u = jnp.where(keep, a * u_prev + u, u)
                a = jnp.where(keep, a * a_prev, a)
            hs = a * carry + u
            hs_ref[rows, sl] = hs
            carry = hs[V7X_SUBLANES - 1:V7X_SUBLANES, :]
        h_ref[:, sl] = carry

    decay = -RG_C * jnp.log(1.0 + jnp.exp(-lam_ref[...]))
    pending = None
    for hh in range(RNN_HEADS):
        sl = slice(hh * RNN_BLOCK, (hh + 1) * RNN_BLOCK)
        xh = xc[:, sl]
        ri = jnp.dot(xh.astype(BF16), wg_ref[hh], preferred_element_type=F32)
        current = (pieces[hh], proj(*pieces[hh]))
        r = _sigmoid(ri[:, :RNN_BLOCK] + ba_ref[:, sl])
        ig = _sigmoid(ri[:, RNN_BLOCK:] + bx_ref[:, sl])
        a = jnp.exp(r * decay[:, sl])
        scan_head(a, jnp.sqrt(1.0 - a * a) * (ig * xh), sl)
        if pending is not None:
            finish(*pending)
        pending = current
    for piece in pieces[RNN_HEADS:]:
        current = (piece, proj(*piece))
        finish(*pending)
        pending = current
    finish(*pending)

    yag_ref[0] = (hs_ref[...] * ext_ref[halo:halo + tm, :]).astype(BF16)


def _in_proj_rglru(layer, x, mod_all, *params):
    b, s, d = x.shape
    tm = TM_IN
    tok_spec = pl.BlockSpec((1, tm, d), lambda bi, ti: (bi, ti, 0))
    act = jax.ShapeDtypeStruct((b, s, d), BF16)
    kern = functools.partial(_in_proj_kernel, tm=tm, d=d)
    return pl.pallas_call(
        kern,
        grid=(b, s // tm),
        in_specs=[
            tok_spec,
            _mod_spec(mod_all, layer),
        ] + [_layer_spec(p, layer) for p in params],
        out_specs=[tok_spec] * 5,
        out_shape=[act] * 5,
        scratch_shapes=[
            pltpu.VMEM((tm + V7X_SUBLANES, d), F32),
            pltpu.VMEM((tm, d), F32),
            pltpu.VMEM((1, d), F32),
        ],
        compiler_params=pltpu.CompilerParams(
            dimension_semantics=("arbitrary", "arbitrary"),
            vmem_limit_bytes=_vmem_limit(60 * 1024 * 1024)),
        name="in_proj_rglru",
    )(x, mod_all, *params)


def _attn_kernel(q_ref, k_ref, v_ref, yag_ref, sgb_ref, sel_ref, o_ref,
                 acc_ref, r_ref, z_ref, p_ref, *, t, nq, heads):
    i = pl.program_id(2)
    row = lax.broadcasted_iota(jnp.int32, (t, t), 0)
    col = lax.broadcasted_iota(jnp.int32, (t, t), 1)
    below = row > col
    tri = below.astype(BF16)
    streams = [(s, g) for s in range(nq) for g in range(heads)]
    rows = [slice(s * t, (s + 1) * t) for s in range(nq)]
    cols = [slice(g * HEAD_DIM, (g + 1) * HEAD_DIM) for g in range(heads)]

    def key_start(j, s):
        return pl.multiple_of(jnp.maximum(j + s, 0) * t, t)

    def scores(j, group=None):
        for u in (range(len(streams)) if group is None else group):
            s, g = streams[u]
            z_ref[u] = lax.dot_general(
                q_ref[0, rows[s], cols[g]], k_ref[0, pl.ds(key_start(j, s), t), cols[g]],
                (((1,), (1,)), ((), ())), preferred_element_type=F32)

    def weighted_values(j):
        for u, (s, g) in enumerate(streams):
            acc_ref[rows[s], cols[g]] += jnp.dot(
                p_ref[u], v_ref[0, pl.ds(key_start(j, s), t), cols[g]],
                preferred_element_type=F32)

    def weights(j, on_diagonal):
        log_betas, sums, totals = {}, {}, None
        n_groups = 4
        per_group = len(streams) // n_groups
        for gi in range(n_groups):
            group = range(gi * per_group, (gi + 1) * per_group)
            parts = []
            for u in group:
                z = z_ref[u]
                log_beta = jnp.minimum(z, 0.0) - jnp.log(1.0 + jnp.exp(-jnp.abs(z)))
                log_1mb = log_beta - z
                if on_diagonal:
                    log_1mb = jnp.where(below, log_1mb, 0.0)
                log_betas[u] = log_beta
                parts.append(log_1mb.astype(BF16))
            group_sums = jnp.dot(jnp.concatenate(parts, axis=0), tri,
                                 preferred_element_type=F32)
            for n, u in enumerate(group):
                sums[u] = group_sums[n * t:(n + 1) * t]
            group_totals = jnp.dot(
                jnp.concatenate(parts, axis=1),
                sel_ref[gi * per_group * t:(gi + 1) * per_group * t, :],
                preferred_element_type=F32)
            totals = group_totals if totals is None else totals + group_totals
            scores(j - 1, group)
        r_old = jnp.where(lane >= -j * heads, r_ref[...], OUT_OF_KEYS)
        for u in range(len(streams)):
            attn = jnp.exp(log_betas[u] + sums[u] + r_old[:, u:u + 1])
            if on_diagonal:
                attn = jnp.where(below, attn, 0.0)
            p_ref[u] = attn.astype(BF16)
        r_new = r_old + totals
        r_ref[...] = r_new
        live = jnp.logical_and(lane < len(streams), r_new >= F32_EXP_ZERO_BELOW)
        return jnp.max(live.astype(jnp.int32))

    lane = lax.broadcasted_iota(jnp.int32, (t, V7X_LANES), 1)
    acc_ref[...] = jnp.zeros_like(acc_ref)
    r_ref[...] = jnp.zeros_like(r_ref)
    j0 = i * nq
    scores(j0)
    live0 = weights(j0, True)

    def cond(c):
        j, live = c
        return jnp.logical_and(j + (nq - 1) >= 0, live > 0)

    def body(c):
        j, _ = c
        weighted_values(j + 1)
        return j - 1, weights(j, False)

    j_end, _ = lax.while_loop(cond, body, (j0 - 1, live0))
    weighted_values(j_end + 1)
    mix = yag_ref[0].astype(F32) + sgb_ref[0].astype(F32) * acc_ref[...]
    o_ref[0] = mix.astype(BF16)


def _stickbreak(q, k, v, yag, sgb):
    b, s, d = q.shape
    t = T_ATT
    nq = ATT_Q_BLOCKS_PER_STEP
    hg = ATT_HEADS_PER_STEP
    kern = functools.partial(_attn_kernel, t=t, nq=nq, heads=hg)
    n_streams = nq * hg
    assert n_streams <= V7X_LANES
    sel = (jnp.arange(n_streams * t)[:, None] // t == jnp.arange(V7X_LANES)[None, :]).astype(BF16)
    q_spec = pl.BlockSpec((1, nq * t, hg * HEAD_DIM), lambda bi, hi, qi: (bi, qi, hi))
    kv_spec = pl.BlockSpec((1, s, hg * HEAD_DIM), lambda bi, hi, qi: (bi, 0, hi))
    return pl.pallas_call(
        kern,
        grid=(b, ATT_HEADS // hg, s // (nq * t)),
        in_specs=[q_spec, kv_spec, kv_spec, q_spec, q_spec, _const_spec(sel.shape)],
        out_specs=q_spec,
        out_shape=jax.ShapeDtypeStruct((b, s, d), BF16),
        scratch_shapes=[
            pltpu.VMEM((nq * t, hg * HEAD_DIM), F32),
            pltpu.VMEM((t, V7X_LANES), F32),
            pltpu.VMEM((nq * hg, t, t), F32),
            pltpu.VMEM((nq * hg, t, t), BF16),
        ],
        compiler_params=pltpu.CompilerParams(
            dimension_semantics=("arbitrary", "arbitrary", "arbitrary"),
            vmem_limit_bytes=_vmem_limit(60 * 1024 * 1024)),
        name="stickbreak",
    )(q, k, v, yag, sgb, sel)


def _mix_ffn_kernel(x_ref, mix_ref, mod_ref, wo_ref, g2_ref, wu_ref,
                    cw_ref, cb_ref, wd_ref, o_ref, ext_ref, hist_ref, act_ref, *, tm, d, dff):
    t_idx = pl.program_id(1)
    halo = V7X_SUBLANES

    @pl.when(t_idx == 0)
    def _():
        hist_ref[...] = jnp.zeros_like(hist_ref)

    gt1 = mod_ref[0, 2:3, :]
    sh2 = mod_ref[0, 3:4, :]
    sc2 = mod_ref[0, 4:5, :]
    gt2 = mod_ref[0, 5:6, :]

    x1 = x_ref[0] + gt1 * jnp.dot(mix_ref[0], wo_ref[...], preferred_element_type=F32)
    h2 = ((x1 * _rms_scale(x1) * g2_ref[...]) * (1.0 + sc2) + sh2).astype(BF16)

    widths = [FF_CHUNK] * (dff // FF_CHUNK)
    starts = [sum(widths[:c]) for c in range(len(widths))]

    def col_slices(c):
        return (slice(starts[c], starts[c] + widths[c]),
                slice(dff + starts[c], dff + starts[c] + widths[c]))

    def up_proj(c):
        for part, cs in enumerate(col_slices(c)):
            buf = 2 * (c % 2) + part
            ext_ref[buf, 0:halo, 0:widths[c]] = hist_ref[:, cs]
            ext_ref[buf, halo:halo + tm, 0:widths[c]] = jnp.dot(h2, wu_ref[:, cs],
                                                                preferred_element_type=F32)
            hist_ref[:, cs] = ext_ref[buf, tm:tm + halo, 0:widths[c]]

    def conv(c, part):
        cs = col_slices(c)[part]
        buf = 2 * (c % 2) + part
        out = cb_ref[:, cs] + cw_ref[FFN_CONV - 1:FFN_CONV, cs] * ext_ref[buf, halo:halo + tm, :]
        for kk in range(FFN_CONV - 1):
            shift = FFN_CONV - 1 - kk
            out = out + cw_ref[kk:kk + 1, cs] * ext_ref[buf, halo - shift:halo - shift + tm, :]
        return out

    def down_proj(c):
        return jnp.dot(act_ref[c % 2, :, 0:widths[c]],
                       wd_ref[starts[c]:starts[c] + widths[c], :],
                       preferred_element_type=F32)

    n_chunks = len(widths)
    acc = jnp.zeros((tm, d), F32)
    up_proj(0)
    for c in range(n_chunks):
        if c + 1 < n_chunks:
            up_proj(c + 1)
        if c > 0:
            acc = acc + down_proj(c - 1)
        act_ref[c % 2, :, 0:widths[c]] = (_gelu_tanh(conv(c, 0)) * conv(c, 1)).astype(BF16)
    acc = acc + down_proj(n_chunks - 1)
    o_ref[0] = x1 + gt2 * acc


def _mix_ffn(layer, x, mix, mod_all, *params):
    b, s, d = x.shape
    dff = params[-1].shape[1]
    tm = TM_FFN
    tok_spec = pl.BlockSpec((1, tm, d), lambda bi, ti: (bi, ti, 0))
    kern = functools.partial(_mix_ffn_kernel, tm=tm, d=d, dff=dff)
    return pl.pallas_call(
        kern,
        grid=(b, s // tm),
        in_specs=[
            tok_spec, tok_spec,
            _mod_spec(mod_all, layer),
        ] + [_layer_spec(p, layer) for p in params],
        out_specs=tok_spec,
        out_shape=jax.ShapeDtypeStruct((b, s, d), F32),
        scratch_shapes=[
            pltpu.VMEM((4, tm + V7X_SUBLANES, FF_CHUNK), F32),
            pltpu.VMEM((V7X_SUBLANES, 2 * dff), F32),
            pltpu.VMEM((2, tm, FF_CHUNK), BF16),
        ],
        compiler_params=pltpu.CompilerParams(
            dimension_semantics=("arbitrary", "arbitrary"),
            vmem_limit_bytes=_vmem_limit(60 * 1024 * 1024)),
        name="mix_ffn",
    )(x, mix, mod_all, *params)


def kernel(x, c, ada_w, ada_b, norm1_g, w_in, conv_w, conv_b, rg_wa, rg_ba, rg_wx, rg_bx,
           rg_lambda, q_norm_g, k_norm_g, w_out, norm2_g, ffn_up, ffn_conv_w, ffn_conv_b,
           ffn_down):
    depth, d, _ = ada_w.shape
    b = x.shape[0]
    for tile in (TM_IN, TM_FFN, T_ATT * ATT_Q_BLOCKS_PER_STEP):
        assert x.shape[1] % tile == 0
    assert d == ATT_HEADS * HEAD_DIM

    mod_all = _adaln_mod(c, ada_w, ada_b).reshape(depth, b, 6, d)

    def rows(p):
        return p.reshape(depth, 1, p.shape[-1])

    in_params = (rows(norm1_g), w_in.astype(BF16), conv_w, rows(conv_b),
                 jnp.concatenate([rg_wa, rg_wx], axis=-1).astype(BF16),
                 rows(rg_ba), rows(rg_bx), rows(rg_lambda), rows(q_norm_g), rows(k_norm_g))
    ffn_params = (w_out.astype(BF16), rows(norm2_g), ffn_up.astype(BF16), ffn_conv_w,
                  rows(ffn_conv_b), ffn_down.astype(BF16))
    for l in range(depth):
        q, k, v, yag, sgb = _in_proj_rglru(l, x, mod_all, *in_params)
        mix = _stickbreak(q, k, v, yag, sgb)
        x = _mix_ffn(l, x, mix, mod_all, *ffn_params)
    return x
```

```python
import functools
import math

import jax
import jax.numpy as jnp
from jax import lax
from jax.experimental import pallas as pl
from jax.experimental.pallas import tpu as pltpu

F32 = jnp.float32
BF16 = jnp.bfloat16

V7X_LANES = 128
V7X_SUBLANES = 8
V7X_VMEM_BYTES = 64 * 1024 * 1024

RNN_HEADS = 8
RNN_BLOCK = 128
RNN_CONV = 4
RG_C = 8.0
ATT_HEADS = 8
HEAD_DIM = 128
FFN_CONV = 3
EPS = 1e-6

F32_EXP_ZERO_BELOW = -105.0
OUT_OF_KEYS = -1e30

TM_IN = 512
TM_FFN = 512
T_ATT = 128
ATT_Q_BLOCKS_PER_STEP = 32
ATT_HEADS_PER_STEP = 1
FF_CHUNK = 512
MOD_TN = 1536


def _vmem_limit(nbytes):
    return int(min(nbytes, V7X_VMEM_BYTES - 4 * 1024 * 1024))


def _const_spec(shape):
    zeros = (0,) * len(shape)
    return pl.BlockSpec(shape, lambda *_: zeros, pipeline_mode=pl.Buffered(1))


def _layer_spec(stacked, layer):
    shape = stacked.shape[1:]
    index = (layer,) + (0,) * len(shape)
    return pl.BlockSpec((None,) + shape, lambda *_: index, pipeline_mode=pl.Buffered(1))


def _mod_spec(mod_all, layer):
    _, _, six, d = mod_all.shape
    return pl.BlockSpec((None, 1, six, d), lambda bi, ti: (layer, bi, 0, 0))


def _gelu_tanh(x):
    c = math.sqrt(2.0 / math.pi)
    return 0.5 * x * (1.0 + jnp.tanh(c * (x + 0.044715 * (x * x * x))))


def _sigmoid(x):
    return 1.0 / (1.0 + jnp.exp(-x))


def _rms_scale(x):
    return lax.rsqrt(jnp.mean(x * x, axis=-1, keepdims=True) + EPS)


def _mod_kernel(c_ref, w_ref, b_ref, o_ref):
    acc = jnp.dot(c_ref[...].astype(BF16), w_ref[0].astype(BF16),
                  preferred_element_type=F32)
    o_ref[0] = acc + b_ref[0]


def _adaln_mod(c, ada_w, ada_b):
    depth, d, n = ada_w.shape
    b = c.shape[0]
    return pl.pallas_call(
        _mod_kernel,
        grid=(depth, n // MOD_TN),
        in_specs=[
            pl.BlockSpec((b, d), lambda l, j: (0, 0)),
            pl.BlockSpec((1, d, MOD_TN), lambda l, j: (l, 0, j)),
            pl.BlockSpec((1, 1, MOD_TN), lambda l, j: (l, 0, j)),
        ],
        out_specs=pl.BlockSpec((1, b, MOD_TN), lambda l, j: (l, 0, j)),
        out_shape=jax.ShapeDtypeStruct((depth, b, n), F32),
        compiler_params=pltpu.CompilerParams(
            dimension_semantics=("arbitrary", "arbitrary"),
            vmem_limit_bytes=_vmem_limit(32 * 1024 * 1024)),
        name="adaln_mod",
    )(c, ada_w, ada_b.reshape(depth, 1, n))


def _in_proj_kernel(x_ref, mod_ref, g1_ref, w_ref, cw_ref, cb_ref, wg_ref, ba_ref, bx_ref,
                    lam_ref, qg_ref, kg_ref,
                    q_ref, k_ref, v_ref, yag_ref, sgb_ref,
                    ext_ref, hs_ref, h_ref, *, tm, d):
    t_idx = pl.program_id(1)
    halo = V7X_SUBLANES

    @pl.when(t_idx == 0)
    def _():
        ext_ref[0:halo, :] = jnp.zeros((halo, d), F32)
        h_ref[...] = jnp.zeros((1, d), F32)

    x = x_ref[0]
    sh1 = mod_ref[0, 0:1, :]
    sc1 = mod_ref[0, 1:2, :]
    h = (x * _rms_scale(x) * g1_ref[...]) * (1.0 + sc1) + sh1
    hb = h.astype(BF16)

    half = d // 2

    def proj(col, part=None):
        c0 = col * d if part is None else col * d + part * half
        c1 = (col + 1) * d if part is None else c0 + half
        return jnp.dot(hb, w_ref[:, c0:c1], preferred_element_type=F32)

    def head_norm(p, gain, o_ref, first_head):
        for hh in range(p.shape[1] // HEAD_DIM):
            ph = p[:, hh * HEAD_DIM:(hh + 1) * HEAD_DIM]
            c0 = (first_head + hh) * HEAD_DIM
            o_ref[0, :, c0:c0 + HEAD_DIM] = (ph * _rms_scale(ph) * gain).astype(BF16)

    ext_ref[halo:halo + tm, :] = proj(0)
    q_proj = proj(2)
    rows = ext_ref[...]
    xc = cb_ref[...] + cw_ref[RNN_CONV - 1:RNN_CONV, :] * rows[halo:halo + tm]
    for kk in range(RNN_CONV - 1):
        shift = RNN_CONV - 1 - kk
        xc = xc + cw_ref[kk:kk + 1, :] * pltpu.roll(rows, shift, axis=0)[halo:halo + tm]
    ext_ref[0:halo, :] = ext_ref[tm:tm + halo, :]
    head_norm(q_proj, qg_ref[...] * (1.0 / math.sqrt(HEAD_DIM)), q_ref, 0)

    pieces = [(3, 0), (3, 1), (4, 0), (4, 1), (6, 0), (6, 1), (1, 0), (5, 0), (1, 1), (5, 1)]
    y_half = {}

    def finish(piece, val):
        col, part = piece
        cs = slice(part * half, (part + 1) * half)
        if col == 3:
            head_norm(val, kg_ref[...], k_ref, part * (ATT_HEADS // 2))
        elif col == 4:
            v_ref[0, :, cs] = val.astype(BF16)
        elif col == 6:
            sgb_ref[0, :, cs] = _sigmoid(val).astype(BF16)
        elif col == 1:
            y_half[part] = _gelu_tanh(val)
        else:
            ext_ref[halo:halo + tm, cs] = y_half[part] * _sigmoid(val)

    row = lax.broadcasted_iota(jnp.int32, (V7X_SUBLANES, RNN_BLOCK), 0)

    def scan_head(a_all, u_all, sl):
        carry = h_ref[:, sl]
        for g in range(tm // V7X_SUBLANES):
            rows = slice(g * V7X_SUBLANES, (g + 1) * V7X_SUBLANES)
            a, u = a_all[rows], u_all[rows]
            for dist in (1, 2, 4):
                keep = row >= dist
                a_prev = pltpu.roll(a, dist, axis=0)
                u_prev = pltpu.roll(u, dist, axis=0)
                u = jnp.where(keep, a * u_prev + u, u)
                a = jnp.where(keep, a * a_prev, a)
            hs = a * carry + u
            hs_ref[rows, sl] = hs
            carry = hs[V7X_SUBLANES - 1:V7X_SUBLANES, :]
        h_ref[:, sl] = carry

    decay = -RG_C * jnp.log(1.0 + jnp.exp(-lam_ref[...]))
    pending = None
    for hh in range(RNN_HEADS):
        sl = slice(hh * RNN_BLOCK, (hh + 1) * RNN_BLOCK)
        xh = xc[:, sl]
        ri = jnp.dot(xh.astype(BF16), wg_ref[hh], preferred_element_type=F32)
        current = (pieces[hh], proj(*pieces[hh]))
        r = _sigmoid(ri[:, :RNN_BLOCK] + ba_ref[:, sl])
        ig = _sigmoid(ri[:, RNN_BLOCK:] + bx_ref[:, sl])
        a = jnp.exp(r * decay[:, sl])
        scan_head(a, jnp.sqrt(1.0 - a * a) * (ig * xh), sl)
        if pending is not None:
            finish(*pending)
        pending = current
    for piece in pieces[RNN_HEADS:]:
        current = (piece, proj(*piece))
        finish(*pending)
        pending = current
    finish(*pending)

    yag_ref[0] = (hs_ref[...] * ext_ref[halo:halo + tm, :]).astype(BF16)


def _in_proj_rglru(layer, x, mod_all, *params):
    b, s, d = x.shape
    tm = TM_IN
    tok_spec = pl.BlockSpec((1, tm, d), lambda bi, ti: (bi, ti, 0))
    act = jax.ShapeDtypeStruct((b, s, d), BF16)
    kern = functools.partial(_in_proj_kernel, tm=tm, d=d)
    return pl.pallas_call(
        kern,
        grid=(b, s // tm),
        in_specs=[
            tok_spec,
            _mod_spec(mod_all, layer),
        ] + [_layer_spec(p, layer) for p in params],
        out_specs=[tok_spec] * 5,
        out_shape=[act] * 5,
        scratch_shapes=[
            pltpu.VMEM((tm + V7X_SUBLANES, d), F32),
            pltpu.VMEM((tm, d), F32),
            pltpu.VMEM((1, d), F32),
        ],
        compiler_params=pltpu.CompilerParams(
            dimension_semantics=("arbitrary", "arbitrary"),
            vmem_limit_bytes=_vmem_limit(60 * 1024 * 1024)),
        name="in_proj_rglru",
    )(x, mod_all, *params)


def _attn_kernel(q_ref, k_ref, v_ref, yag_ref, sgb_ref, sel_ref, o_ref,
                 acc_ref, r_ref, z_ref, p_ref, *, t, nq, heads):
    i = pl.program_id(2)
    row = lax.broadcasted_iota(jnp.int32, (t, t), 0)
    col = lax.broadcasted_iota(jnp.int32, (t, t), 1)
    below = row > col
    tri = below.astype(BF16)
    streams = [(s, g) for s in range(nq) for g in range(heads)]
    rows = [slice(s * t, (s + 1) * t) for s in range(nq)]
    cols = [slice(g * HEAD_DIM, (g + 1) * HEAD_DIM) for g in range(heads)]

    def key_start(j, s):
        return pl.multiple_of(jnp.maximum(j + s, 0) * t, t)

    def scores(j, group=None):
        for u in (range(len(streams)) if group is None else group):
            s, g = streams[u]
            z_ref[u] = lax.dot_general(
                q_ref[0, rows[s], cols[g]], k_ref[0, pl.ds(key_start(j, s), t), cols[g]],
                (((1,), (1,)), ((), ())), preferred_element_type=F32)

    def weighted_values(j):
        for u, (s, g) in enumerate(streams):
            acc_ref[rows[s], cols[g]] += jnp.dot(
                p_ref[u], v_ref[0, pl.ds(key_start(j, s), t), cols[g]],
                preferred_element_type=F32)

    def weights(j, on_diagonal):
        log_betas, sums, totals = {}, {}, None
        n_groups = 4
        per_group = len(streams) // n_groups
        for gi in range(n_groups):
            group = range(gi * per_group, (gi + 1) * per_group)
            parts = []
            for u in group:
                z = z_ref[u]
                log_beta = jnp.minimum(z, 0.0) - jnp.log(1.0 + jnp.exp(-jnp.abs(z)))
                log_1mb = log_beta - z
                if on_diagonal:
                    log_1mb = jnp.where(below, log_1mb, 0.0)
                log_betas[u] = log_beta
                parts.append(log_1mb.astype(BF16))
            group_sums = jnp.dot(jnp.concatenate(parts, axis=0), tri,
                                 preferred_element_type=F32)
            for n, u in enumerate(group):
                sums[u] = group_sums[n * t:(n + 1) * t]
            group_totals = jnp.dot(
                jnp.concatenate(parts, axis=1),
                sel_ref[gi * per_group * t:(gi + 1) * per_group * t, :],
                preferred_element_type=F32)
            totals = group_totals if totals is None else totals + group_totals
            scores(j - 1, group)
        r_old = jnp.where(lane >= -j * heads, r_ref[...], OUT_OF_KEYS)
        for u in range(len(streams)):
            attn = jnp.exp(log_betas[u] + sums[u] + r_old[:, u:u + 1])
            if on_diagonal:
                attn = jnp.where(below, attn, 0.0)
            p_ref[u] = attn.astype(BF16)
        r_new = r_old + totals
        r_ref[...] = r_new
        live = jnp.logical_and(lane < len(streams), r_new >= F32_EXP_ZERO_BELOW)
        return jnp.max(live.astype(jnp.int32))

    lane = lax.broadcasted_iota(jnp.int32, (t, V7X_LANES), 1)
    acc_ref[...] = jnp.zeros_like(acc_ref)
    r_ref[...] = jnp.zeros_like(r_ref)
    j0 = i * nq
    scores(j0)
    live0 = weights(j0, True)

    def cond(c):
        j, live = c
        return jnp.logical_and(j + (nq - 1) >= 0, live > 0)

    def body(c):
        j, _ = c
        weighted_values(j + 1)
        return j - 1, weights(j, False)

    j_end, _ = lax.while_loop(cond, body, (j0 - 1, live0))
    weighted_values(j_end + 1)
    mix = yag_ref[0].astype(F32) + sgb_ref[0].astype(F32) * acc_ref[...]
    o_ref[0] = mix.astype(BF16)


def _stickbreak(q, k, v, yag, sgb):
    b, s, d = q.shape
    t = T_ATT
    nq = ATT_Q_BLOCKS_PER_STEP
    hg = ATT_HEADS_PER_STEP
    kern = functools.partial(_attn_kernel, t=t, nq=nq, heads=hg)
    n_streams = nq * hg
    assert n_streams <= V7X_LANES
    sel = (jnp.arange(n_streams * t)[:, None] // t == jnp.arange(V7X_LANES)[None, :]).astype(BF16)
    q_spec = pl.BlockSpec((1, nq * t, hg * HEAD_DIM), lambda bi, hi, qi: (bi, qi, hi))
    kv_spec = pl.BlockSpec((1, s, hg * HEAD_DIM), lambda bi, hi, qi: (bi, 0, hi))
    return pl.pallas_call(
        kern,
        grid=(b, ATT_HEADS // hg, s // (nq * t)),
        in_specs=[q_spec, kv_spec, kv_spec, q_spec, q_spec, _const_spec(sel.shape)],
        out_specs=q_spec,
        out_shape=jax.ShapeDtypeStruct((b, s, d), BF16),
        scratch_shapes=[
            pltpu.VMEM((nq * t, hg * HEAD_DIM), F32),
            pltpu.VMEM((t, V7X_LANES), F32),
            pltpu.VMEM((nq * hg, t, t), F32),
            pltpu.VMEM((nq * hg, t, t), BF16),
        ],
        compiler_params=pltpu.CompilerParams(
            dimension_semantics=("arbitrary", "arbitrary", "arbitrary"),
            vmem_limit_bytes=_vmem_limit(60 * 1024 * 1024)),
        name="stickbreak",
    )(q, k, v, yag, sgb, sel)


def _mix_ffn_kernel(x_ref, mix_ref, mod_ref, wo_ref, g2_ref, wu_ref,
                    cw_ref, cb_ref, wd_ref, o_ref, ext_ref, hist_ref, act_ref, *, tm, d, dff):
    t_idx = pl.program_id(1)
    halo = V7X_SUBLANES

    @pl.when(t_idx == 0)
    def _():
        hist_ref[...] = jnp.zeros_like(hist_ref)

    gt1 = mod_ref[0, 2:3, :]
    sh2 = mod_ref[0, 3:4, :]
    sc2 = mod_ref[0, 4:5, :]
    gt2 = mod_ref[0, 5:6, :]

    x1 = x_ref[0] + gt1 * jnp.dot(mix_ref[0], wo_ref[...], preferred_element_type=F32)
    h2 = ((x1 * _rms_scale(x1) * g2_ref[...]) * (1.0 + sc2) + sh2).astype(BF16)

    widths = [FF_CHUNK] * (dff // FF_CHUNK)
    starts = [sum(widths[:c]) for c in range(len(widths))]

    def col_slices(c):
        return (slice(starts[c], starts[c] + widths[c]),
                slice(dff + starts[c], dff + starts[c] + widths[c]))

    def up_proj(c):
        for part, cs in enumerate(col_slices(c)):
            buf = 2 * (c % 2) + part
            ext_ref[buf, 0:halo, 0:widths[c]] = hist_ref[:, cs]
            ext_ref[buf, halo:halo + tm, 0:widths[c]] = jnp.dot(h2, wu_ref[:, cs],
                                                                preferred_element_type=F32)
            hist_ref[:, cs] = ext_ref[buf, tm:tm + halo, 0:widths[c]]

    def conv(c, part):
        cs = col_slices(c)[part]
        buf = 2 * (c % 2) + part
        out = cb_ref[:, cs] + cw_ref[FFN_CONV - 1:FFN_CONV, cs] * ext_ref[buf, halo:halo + tm, :]
        for kk in range(FFN_CONV - 1):
            shift = FFN_CONV - 1 - kk
            out = out + cw_ref[kk:kk + 1, cs] * ext_ref[buf, halo - shift:halo - shift + tm, :]
        return out

    def down_proj(c):
        return jnp.dot(act_ref[c % 2, :, 0:widths[c]],
                       wd_ref[starts[c]:starts[c] + widths[c], :],
                       preferred_element_type=F32)

    n_chunks = len(widths)
    acc = jnp.zeros((tm, d), F32)
    up_proj(0)
    for c in range(n_chunks):
        if c + 1 < n_chunks:
            up_proj(c + 1)
        if c > 0:
            acc = acc + down_proj(c - 1)
        act_ref[c % 2, :, 0:widths[c]] = (_gelu_tanh(conv(c, 0)) * conv(c, 1)).astype(BF16)
    acc = acc + down_proj(n_chunks - 1)
    o_ref[0] = x1 + gt2 * acc


def _mix_ffn(layer, x, mix, mod_all, *params):
    b, s, d = x.shape
    dff = params[-1].shape[1]
    tm = TM_FFN
    tok_spec = pl.BlockSpec((1, tm, d), lambda bi, ti: (bi, ti, 0))
    kern = functools.partial(_mix_ffn_kernel, tm=tm, d=d, dff=dff)
    return pl.pallas_call(
        kern,
        grid=(b, s // tm),
        in_specs=[
            tok_spec, tok_spec,
            _mod_spec(mod_all, layer),
        ] + [_layer_spec(p, layer) for p in params],
        out_specs=tok_spec,
        out_shape=jax.ShapeDtypeStruct((b, s, d), F32),
        scratch_shapes=[
            pltpu.VMEM((4, tm + V7X_SUBLANES, FF_CHUNK), F32),
            pltpu.VMEM((V7X_SUBLANES, 2 * dff), F32),
            pltpu.VMEM((2, tm, FF_CHUNK), BF16),
        ],
        compiler_params=pltpu.CompilerParams(
            dimension_semantics=("arbitrary", "arbitrary"),
            vmem_limit_bytes=_vmem_limit(60 * 1024 * 1024)),
        name="mix_ffn",
    )(x, mix, mod_all, *params)


def kernel(x, c, ada_w, ada_b, norm1_g, w_in, conv_w, conv_b, rg_wa, rg_ba, rg_wx, rg_bx,
           rg_lambda, q_norm_g, k_norm_g, w_out, norm2_g, ffn_up, ffn_conv_w, ffn_conv_b,
           ffn_down):
    depth, d, _ = ada_w.shape
    b = x.shape[0]
    for tile in (TM_IN, TM_FFN, T_ATT * ATT_Q_BLOCKS_PER_STEP):
        assert x.shape[1] % tile == 0
    assert d == ATT_HEADS * HEAD_DIM

    mod_all = _adaln_mod(c, ada_w, ada_b).reshape(depth, b, 6, d)

    def rows(p):
        return p.reshape(depth, 1, p.shape[-1])

    in_params = (rows(norm1_g), w_in.astype(BF16), conv_w, rows(conv_b),
                 jnp.concatenate([rg_wa, rg_wx], axis=-1).astype(BF16),
                 rows(rg_ba), rows(rg_bx), rows(rg_lambda), rows(q_norm_g), rows(k_norm_g))
    ffn_params = (w_out.astype(BF16), rows(norm2_g), ffn_up.astype(BF16), ffn_conv_w,
                  rows(ffn_conv_b), ffn_down.astype(BF16))
    for l in range(depth):
        q, k, v, yag, sgb = _in_proj_rglru(l, x, mod_all, *in_params)
        mix = _stickbreak(q, k, v, yag, sgb)
        x = _mix_ffn(l, x, mix, mod_all, *ffn_params)
    return x
```
